```python
import jax, jax.numpy as jnp
from jax import lax
import numpy as np

D_MODEL = 1024
BATCH = 2
SEQ = 8192
DEPTH = 2

CHUNK = 64
MIX_WIDTH = D_MODEL
CONV_WIDTH_CH = MIX_WIDTH // 2
CONV_HEADS = 8
CONV_K = 3
POOL_WIDTH_CH = MIX_WIDTH - CONV_WIDTH_CH
POOL_WINDOWS = (2, 4, 8, 16)
N_POOL_GROUPS = len(POOL_WINDOWS)
POOL_GC = POOL_WIDTH_CH // N_POOL_GROUPS
IN_COLS = 4 * CONV_WIDTH_CH + 2 * POOL_WIDTH_CH
RMS_EPS = 1e-6

kernel_name = "hybrid_shortconv_pool_sandwich"


def rmsnorm(x, g):
    xf = x.astype(jnp.float32)
    inv = lax.rsqrt(jnp.mean(xf * xf, axis=-1, keepdims=True) + RMS_EPS)
    return (xf * inv).astype(x.dtype) * g


def causal_depthwise_conv3(v, w, b):
    S = v.shape[1]
    vp = jnp.pad(v, ((0, 0), (CONV_K - 1, 0), (0, 0)))
    y = w[0] * vp[:, 0:S] + w[1] * vp[:, 1:S + 1] + w[2] * vp[:, 2:S + 2]
    return y + b


def multiscale_pool(u, w_pool, scale):
    Bz, S, C = u.shape
    uf = u.astype(jnp.float32)
    cs = jnp.cumsum(uf, axis=1)
    count_pos = jnp.arange(1, S + 1, dtype=jnp.float32)[None, :, None]
    outs = []
    for g, w in enumerate(POOL_WINDOWS):
        sl = slice(g * POOL_GC, (g + 1) * POOL_GC)
        csg = cs[..., sl]
        prev = jnp.pad(csg, ((0, 0), (w, 0), (0, 0)))[:, :S]
        mean = (csg - prev) / jnp.minimum(count_pos, float(w))
        outs.append(mean - uf[..., sl])
    p = jnp.concatenate(outs, axis=-1).astype(u.dtype).reshape(Bz, S, N_POOL_GROUPS, POOL_GC)
    y = jnp.einsum('bsgc,gcd->bsgd', p, w_pool).reshape(Bz, S, C)
    return y * scale


def setup_inputs(seed: int = 0) -> dict:
    key = jax.random.key(seed)
    ks = jax.random.split(key, 10)
    f32 = jnp.float32
    x = jax.random.normal(ks[0], (BATCH, SEQ, D_MODEL), f32)
    pre_norm = 1.0 + 0.05 * jax.random.normal(ks[1], (DEPTH, D_MODEL), f32)
    w_in = jax.random.normal(ks[2], (DEPTH, D_MODEL, IN_COLS), f32) * D_MODEL ** -0.5
    conv_w = jax.random.normal(ks[3], (DEPTH, CONV_K, CONV_WIDTH_CH), f32) * CONV_K ** -0.5
    conv_b = 0.02 * jax.random.normal(ks[4], (DEPTH, CONV_WIDTH_CH), f32)
    w_pool = jax.random.normal(ks[5], (DEPTH, N_POOL_GROUPS, POOL_GC, POOL_GC), f32) * POOL_GC ** -0.5
    pool_scale = 1.0 + 0.05 * jax.random.normal(ks[6], (DEPTH, POOL_WIDTH_CH), f32)
    w_out = jax.random.normal(ks[7], (DEPTH, MIX_WIDTH, D_MODEL), f32) * MIX_WIDTH ** -0.5
    post_norm = 1.0 + 0.05 * jax.random.normal(ks[8], (DEPTH, D_MODEL), f32)
    return {"x": x, "pre_norm": pre_norm, "w_in": w_in, "conv_w": conv_w, "conv_b": conv_b,
            "w_pool": w_pool, "pool_scale": pool_scale, "w_out": w_out, "post_norm": post_norm}


def reference(x, pre_norm, w_in, conv_w, conv_b, w_pool, pool_scale, w_out, post_norm):
    A = CONV_WIDTH_CH
    P = POOL_WIDTH_CH
    for l in range(DEPTH):
        hn = rmsnorm(x, pre_norm[l])
        proj = jnp.einsum('bsd,dc->bsc', hn, w_in[l])
        b_a = proj[..., 0:A]
        c_a = proj[..., A:2 * A]
        h_a = proj[..., 2 * A:3 * A]
        z_a = proj[..., 3 * A:4 * A]
        u_b = proj[..., 4 * A:4 * A + P]
        z_b = proj[..., 4 * A + P:4 * A + 2 * P]
        y_a = b_a * causal_depthwise_conv3(c_a * h_a, conv_w[l], conv_b[l])
        y_a = y_a * jax.nn.silu(z_a)
        y_b = multiscale_pool(u_b, w_pool[l], pool_scale[l]) * jax.nn.silu(z_b)
        y = jnp.concatenate([y_a, y_b], axis=-1)
        out = jnp.einsum('bsc,cd->bsd', y, w_out[l])
        x = x + rmsnorm(out, post_norm[l])
    return x
```

```python
import functools

import jax
import jax.numpy as jnp
from jax import lax
from jax.experimental import pallas as pl
from jax.experimental.pallas import tpu as pltpu

RMS_EPS = 1e-6
CONV_K = 3
POOL_WINDOWS = (2, 4, 8, 16)
N_POOL_GROUPS = len(POOL_WINDOWS)

SEQ_TILE = 512
CONV_HALO = 8
POOL_HALO = 16
VMEM_LIMIT_BYTES = 48 * 1024 * 1024


def _rms_inv(v):
    return lax.rsqrt(jnp.mean(v * v, axis=-1, keepdims=True) + RMS_EPS)


def _silu(z):
    return z * jax.nn.sigmoid(z)


def _layer_kernel(x_ref, pre_ref, win_ref, cw_ref, cb_ref, wp_ref, ps_ref, wout_ref, post_ref,
                  o_ref, vext_ref, uext_ref, y_ref, *, seq_tile, conv_ch, pool_gc):
    ts = seq_tile
    a = conv_ch
    j = pl.program_id(1)

    @pl.when(j == 0)
    def _():
        vext_ref[0:CONV_HALO, :] = jnp.zeros((CONV_HALO, a), jnp.float32)
        uext_ref[0:POOL_HALO, :] = jnp.zeros((POOL_HALO, uext_ref.shape[1]), jnp.float32)

    x = x_ref[0]
    hn = ((x * _rms_inv(x)) * pre_ref[...]).astype(jnp.bfloat16)

    def proj(lo, width):
        return jnp.dot(hn, win_ref[:, lo:lo + width], preferred_element_type=jnp.float32)

    v = proj(a, a) * proj(2 * a, a)
    vext_ref[CONV_HALO:CONV_HALO + ts, :] = v
    conv = (cw_ref[0:1, :] * vext_ref[CONV_HALO - 2:CONV_HALO - 2 + ts, :]
            + cw_ref[1:2, :] * vext_ref[CONV_HALO - 1:CONV_HALO - 1 + ts, :]
            + cw_ref[2:3, :] * v) + cb_ref[...]
    vext_ref[0:CONV_HALO, :] = vext_ref[ts:ts + CONV_HALO, :]
    y_a = (proj(0, a) * conv) * _silu(proj(3 * a, a))
    y_ref[:, 0:a] = y_a.astype(jnp.bfloat16)

    u = proj(4 * a, uext_ref.shape[1])
    p_ch = uext_ref.shape[1]
    uext_ref[POOL_HALO:POOL_HALO + ts, :] = u
    row = lax.broadcasted_iota(jnp.int32, (POOL_HALO, pool_gc), 0)
    pos1 = (j * ts + row + 1).astype(jnp.float32)
    for g, w in enumerate(POOL_WINDOWS):
        lo = g * pool_gc
        ug = u[:, lo:lo + pool_gc]
        s = ug
        for k in range(1, w):
            s = s + uext_ref[POOL_HALO - k:POOL_HALO - k + ts, lo:lo + pool_gc]
        p_body = s * (1.0 / w) - ug
        p_head = s[0:POOL_HALO] / jnp.minimum(pos1, float(w)) - ug[0:POOL_HALO]
        p = jnp.concatenate([p_head, p_body[POOL_HALO:]], axis=0).astype(jnp.bfloat16)
        yg = jnp.dot(p, wp_ref[g], preferred_element_type=jnp.float32)
        yg = (yg * ps_ref[:, lo:lo + pool_gc]) * _silu(proj(4 * a + p_ch + lo, pool_gc))
        y_ref[:, a + lo:a + lo + pool_gc] = yg.astype(jnp.bfloat16)
    uext_ref[0:POOL_HALO, :] = uext_ref[ts:ts + POOL_HALO, :]

    out = jnp.dot(y_ref[...], wout_ref[...], preferred_element_type=jnp.float32)
    o_ref[0] = x + (out * _rms_inv(out)) * post_ref[...]


def _layer(x, pre, win, cw, cb, wp, ps, wout, post):
    bsz, seq, d = x.shape
    a = cw.shape[1]
    p_ch = ps.shape[1]
    pool_gc = wp.shape[1]
    in_cols = win.shape[1]
    ts = SEQ_TILE
    assert seq % ts == 0 and in_cols == 4 * a + 2 * p_ch and p_ch == N_POOL_GROUPS * pool_gc

    const2 = lambda b, j: (0, 0)
    kernel = functools.partial(_layer_kernel, seq_tile=ts, conv_ch=a, pool_gc=pool_gc)
    return pl.pallas_call(
        kernel,
        grid=(bsz, seq // ts),
        in_specs=[
            pl.BlockSpec((1, ts, d), lambda b, j: (b, j, 0)),
            pl.BlockSpec((1, d), const2),
            pl.BlockSpec((d, in_cols), const2),
            pl.BlockSpec((CONV_K, a), const2),
            pl.BlockSpec((1, a), const2),
            pl.BlockSpec((N_POOL_GROUPS, pool_gc, pool_gc), lambda b, j: (0, 0, 0)),
            pl.BlockSpec((1, p_ch), const2),
            pl.BlockSpec((a + p_ch, d), const2),
            pl.BlockSpec((1, d), const2),
        ],
        out_specs=pl.BlockSpec((1, ts, d), lambda b, j: (b, j, 0)),
        out_shape=jax.ShapeDtypeStruct(x.shape, x.dtype),
        scratch_shapes=[
            pltpu.VMEM((ts + CONV_HALO, a), jnp.float32),
            pltpu.VMEM((ts + POOL_HALO, p_ch), jnp.float32),
            pltpu.VMEM((ts, a + p_ch), jnp.bfloat16),
        ],
        compiler_params=pltpu.CompilerParams(
            dimension_semantics=("arbitrary", "arbitrary"),
            vmem_limit_bytes=VMEM_LIMIT_BYTES),
        name="sandwich_layer",
    )(x, pre, win, cw, cb, wp, ps, wout, post)


def kernel(x, pre_norm, w_in, conv_w, conv_b, w_pool, pool_scale, w_out, post_norm):
    depth = w_in.shape[0]
    for l in range(depth):
        x = _layer(
            x,
            pre_norm[l][None, :],
            w_in[l].astype(jnp.bfloat16),
            conv_w[l],
            conv_b[l][None, :],
            w_pool[l].astype(jnp.bfloat16),
            pool_scale[l][None, :],
            w_out[l].astype(jnp.bfloat16),
            post_norm[l][None, :],
        )
    return x
```

```python
import functools

import jax
import jax.numpy as jnp
from jax import lax
from jax.experimental import pallas as pl
from jax.experimental.pallas import tpu as pltpu

RMS_EPS = 1e-6
CONV_K = 3
POOL_WINDOWS = (2, 4, 8, 16)
N_POOL_GROUPS = len(POOL_WINDOWS)

SEQ_TILE = 512
OUT_ROWS = 256
CONV_HALO = 8
POOL_HALO = 16
VMEM_LIMIT_BYTES = 48 * 1024 * 1024


def _rms_inv(v):
    return lax.rsqrt(jnp.mean(v * v, axis=-1, keepdims=True) + RMS_EPS)


def _silu(z):
    return z * jax.nn.sigmoid(z)


def _shift_rows(v, k):
    if k % 8 == 0:
        return jnp.concatenate([v[-k:], v[:-k]], axis=0)
    return pltpu.roll(v, k, axis=0)


def _layer_kernel(x_ref, pre_ref, win_ref, cw_ref, cb_ref, wp_ref, ps_ref, wout_ref, post_ref,
                  o_ref, vext_ref, uext_ref, p_ref, y_ref, *, seq_tile, conv_ch, pool_gc):
    ts = seq_tile
    a = conv_ch
    p_ch = uext_ref.shape[1]
    j = pl.program_id(1)

    @pl.when(j == 0)
    def _():
        vext_ref[0:CONV_HALO, :] = jnp.zeros((CONV_HALO, a), jnp.float32)
        uext_ref[0:POOL_HALO, :] = jnp.zeros((POOL_HALO, p_ch), jnp.float32)

    x = x_ref[0]
    xg = (x * pre_ref[...]).astype(jnp.bfloat16)
    inv = _rms_inv(x)

    def proj(lo, width):
        return jnp.dot(xg, win_ref[:, lo:lo + width], preferred_element_type=jnp.float32)

    v = (proj(a, a) * proj(2 * a, a)) * (inv * inv)
    vext_ref[CONV_HALO:CONV_HALO + ts, :] = v
    vx = vext_ref[...]
    conv = (cw_ref[0:1, :] * _shift_rows(vx, 2)[CONV_HALO:]
            + cw_ref[1:2, :] * _shift_rows(vx, 1)[CONV_HALO:]
            + cw_ref[2:3, :] * v) + cb_ref[...]
    vext_ref[0:CONV_HALO, :] = v[ts - CONV_HALO:]
    y_a = ((proj(0, a) * inv) * conv) * _silu(proj(3 * a, a) * inv)
    y_ref[:, 0:a] = y_a.astype(jnp.bfloat16)

    u = proj(4 * a, p_ch) * inv
    uext_ref[POOL_HALO:POOL_HALO + ts, :] = u
    row = lax.broadcasted_iota(jnp.int32, (POOL_HALO, pool_gc), 0)
    pos1 = (j * ts + row + 1).astype(jnp.float32)
    for g, w in enumerate(POOL_WINDOWS):
        lo = g * pool_gc
        ug = u[:, lo:lo + pool_gc]
        s = uext_ref[:, lo:lo + pool_gc]
        m = 1
        while m < w:
            s = s + _shift_rows(s, m)
            m *= 2
        s = s[POOL_HALO:]
        p_body = s * (1.0 / w) - ug
        p_head = s[0:POOL_HALO] / jnp.minimum(pos1, float(w)) - ug[0:POOL_HALO]
        p = jnp.concatenate([p_head, p_body[POOL_HALO:]], axis=0)
        p_ref[:, lo:lo + pool_gc] = p.astype(jnp.bfloat16)
    uext_ref[0:POOL_HALO, :] = u[ts - POOL_HALO:]
    gate_b = _silu(proj(4 * a + p_ch, p_ch) * inv) * ps_ref[...]
    for h in range(wp_ref.shape[0]):
        lo = h * 2 * pool_gc
        yb = jnp.dot(p_ref[:, lo:lo + 2 * pool_gc], wp_ref[h], preferred_element_type=jnp.float32)
        y_ref[:, a + lo:a + lo + 2 * pool_gc] = (yb * gate_b[:, lo:lo + 2 * pool_gc]).astype(jnp.bfloat16)

    for r in range(0, ts, OUT_ROWS):
        out = jnp.dot(y_ref[r:r + OUT_ROWS, :], wout_ref[...], preferred_element_type=jnp.float32)
        o_ref[0, r:r + OUT_ROWS, :] = x_ref[0, r:r + OUT_ROWS, :] + (out * _rms_inv(out)) * post_ref[...]


def _layer(l, x, pre, win, cw, cb, wp, ps, wout, post):
    bsz, seq, d = x.shape
    a = cw.shape[-1]
    p_ch = ps.shape[-1]
    pool_gc = wp.shape[-1] // 2
    in_cols = win.shape[-1]
    ts = SEQ_TILE
    assert seq % ts == 0 and ts % OUT_ROWS == 0
    assert in_cols == 4 * a + 2 * p_ch and p_ch == N_POOL_GROUPS * pool_gc

    def layer_block(*shape):
        return pl.BlockSpec((None,) + shape, lambda b, j: (l,) + (0,) * len(shape))

    kernel = functools.partial(_layer_kernel, seq_tile=ts, conv_ch=a, pool_gc=pool_gc)
    return pl.pallas_call(
        kernel,
        grid=(bsz, seq // ts),
        in_specs=[
            pl.BlockSpec((1, ts, d), lambda b, j: (b, j, 0)),
            layer_block(1, d),
            layer_block(d, in_cols),
            layer_block(CONV_K, a),
            layer_block(1, a),
            layer_block(N_POOL_GROUPS // 2, 2 * pool_gc, 2 * pool_gc),
            layer_block(1, p_ch),
            layer_block(a + p_ch, d),
            layer_block(1, d),
        ],
        out_specs=pl.BlockSpec((1, ts, d), lambda b, j: (b, j, 0)),
        out_shape=jax.ShapeDtypeStruct(x.shape, x.dtype),
        scratch_shapes=[
            pltpu.VMEM((CONV_HALO + ts, a), jnp.float32),
            pltpu.VMEM((POOL_HALO + ts, p_ch), jnp.float32),
            pltpu.VMEM((ts, p_ch), jnp.bfloat16),
            pltpu.VMEM((ts, a + p_ch), jnp.bfloat16),
        ],
        compiler_params=pltpu.CompilerParams(
            dimension_semantics=("arbitrary", "arbitrary"),
            vmem_limit_bytes=VMEM_LIMIT_BYTES),
        name="sandwich_layer",
    )(x, pre, win, cw, cb, wp, ps, wout, post)


def _pair_block_diag(w_pool):
    depth, groups, c, _ = w_pool.shape
    wp = w_pool.reshape(depth, groups // 2, 2, c, c)
    z = jnp.zeros_like(wp[:, :, 0])
    top = jnp.concatenate([wp[:, :, 0], z], axis=-1)
    bot = jnp.concatenate([z, wp[:, :, 1]], axis=-1)
    return jnp.concatenate([top, bot], axis=-2)


def kernel(x, pre_norm, w_in, conv_w, conv_b, w_pool, pool_scale, w_out, post_norm):
    depth = w_in.shape[0]
    win = w_in.astype(jnp.bfloat16)
    wout = w_out.astype(jnp.bfloat16)
    wp = _pair_block_diag(w_pool).astype(jnp.bfloat16)
    pre = pre_norm[:, None, :]
    cb = conv_b[:, None, :]
    ps = pool_scale[:, None, :]
    post = post_norm[:, None, :]
    for l in range(depth):
        x = _layer(l, x, pre, win, conv_w, cb, wp, ps, wout, post)
    return x
```

```python
import functools

import jax
import jax.numpy as jnp
from jax import lax
from jax.experimental import pallas as pl
from jax.experimental.pallas import tpu as pltpu

RMS_EPS = 1e-6
CONV_K = 3
POOL_WINDOWS = (2, 4, 8, 16)
N_POOL_GROUPS = len(POOL_WINDOWS)

SEQ_TILE = 1024
OUT_ROWS = 256
CONV_HALO = 8
POOL_HALO = 16
VMEM_LIMIT_BYTES = 56 * 1024 * 1024


def _rms_inv(v):
    return lax.rsqrt(jnp.mean(v * v, axis=-1, keepdims=True) + RMS_EPS)


def _silu(z):
    return z * jax.nn.sigmoid(z)


def _shift_rows(v, k):
    if k % 8 == 0:
        return jnp.concatenate([v[-k:], v[:-k]], axis=0)
    return pltpu.roll(v, k, axis=0)


def _layer_kernel(x_ref, pre_ref, win_ref, cw_ref, cb_ref, wp_ref, ps_ref, wout_ref, post_ref,
                  o_ref, vext_ref, uext_ref, p_ref, y_ref, *, seq_tile, conv_ch, pool_gc):
    ts = seq_tile
    a = conv_ch
    p_ch = uext_ref.shape[1]
    j = pl.program_id(1)

    @pl.when(j == 0)
    def _():
        vext_ref[0:CONV_HALO, :] = jnp.zeros((CONV_HALO, a), jnp.float32)
        uext_ref[0:POOL_HALO, :] = jnp.zeros((POOL_HALO, p_ch), jnp.float32)

    x = x_ref[0]
    xg = (x * pre_ref[...]).astype(jnp.bfloat16)
    inv = _rms_inv(x)

    def proj(lo, width):
        return jnp.dot(xg, win_ref[:, lo:lo + width], preferred_element_type=jnp.float32)

    v = (proj(a, a) * proj(2 * a, a)) * (inv * inv)
    vext_ref[CONV_HALO:CONV_HALO + ts, :] = v
    vx = vext_ref[...]
    conv = (cw_ref[0:1, :] * _shift_rows(vx, 2)[CONV_HALO:]
            + cw_ref[1:2, :] * _shift_rows(vx, 1)[CONV_HALO:]
            + cw_ref[2:3, :] * v) + cb_ref[...]
    vext_ref[0:CONV_HALO, :] = v[ts - CONV_HALO:]
    y_a = ((proj(0, a) * inv) * conv) * _silu(proj(3 * a, a) * inv)
    y_ref[:, 0:a] = y_a.astype(jnp.bfloat16)

    u = proj(4 * a, p_ch) * inv
    uext_ref[POOL_HALO:POOL_HALO + ts, :] = u
    row = lax.broadcasted_iota(jnp.int32, (POOL_HALO, pool_gc), 0)
    pos1 = (j * ts + row + 1).astype(jnp.float32)
    for g, w in enumerate(POOL_WINDOWS):
        lo = g * pool_gc
        ug = u[:, lo:lo + pool_gc]
        s = uext_ref[:, lo:lo + pool_gc]
        m = 1
        while m < w:
            s = s + _shift_rows(s, m)
            m *= 2
        s = s[POOL_HALO:]
        p_body = s * (1.0 / w) - ug
        p_head = s[0:POOL_HALO] / jnp.minimum(pos1, float(w)) - ug[0:POOL_HALO]
        p = jnp.concatenate([p_head, p_body[POOL_HALO:]], axis=0)
        p_ref[:, lo:lo + pool_gc] = p.astype(jnp.bfloat16)
    uext_ref[0:POOL_HALO, :] = u[ts - POOL_HALO:]
    gate_b = _silu(proj(4 * a + p_ch, p_ch) * inv) * ps_ref[...]
    for h in range(wp_ref.shape[0]):
        lo = h * 2 * pool_gc
        yb = jnp.dot(p_ref[:, lo:lo + 2 * pool_gc], wp_ref[h], preferred_element_type=jnp.float32)
        y_ref[:, a + lo:a + lo + 2 * pool_gc] = (yb * gate_b[:, lo:lo + 2 * pool_gc]).astype(jnp.bfloat16)

    for r in range(0, ts, OUT_ROWS):
        out = jnp.dot(y_ref[r:r + OUT_ROWS, :], wout_ref[...], preferred_element_type=jnp.float32)
        o_ref[0, r:r + OUT_ROWS, :] = x_ref[0, r:r + OUT_ROWS, :] + (out * _rms_inv(out)) * post_ref[...]


def _layer(l, x, pre, win, cw, cb, wp, ps, wout, post):
    bsz, seq, d = x.shape
    a = cw.shape[-1]
    p_ch = ps.shape[-1]
    pool_gc = wp.shape[-1] // 2
    in_cols = win.shape[-1]
    ts = SEQ_TILE
    assert seq % ts == 0 and ts % OUT_ROWS == 0
    assert in_cols == 4 * a + 2 * p_ch and p_ch == N_POOL_GROUPS * pool_gc

    def layer_block(*shape):
        return pl.BlockSpec((None,) + shape, lambda b, j: (l,) + (0,) * len(shape))

    kernel = functools.partial(_layer_kernel, seq_tile=ts, conv_ch=a, pool_gc=pool_gc)
    return pl.pallas_call(
        kernel,
        grid=(bsz, seq // ts),
        in_specs=[
            pl.BlockSpec((1, ts, d), lambda b, j: (b, j, 0)),
            layer_block(1, d),
            layer_block(d, in_cols),
            layer_block(CONV_K, a),
            layer_block(1, a),
            layer_block(N_POOL_GROUPS // 2, 2 * pool_gc, 2 * pool_gc),
            layer_block(1, p_ch),
            layer_block(a + p_ch, d),
            layer_block(1, d),
        ],
        out_specs=pl.BlockSpec((1, ts, d), lambda b, j: (b, j, 0)),
        out_shape=jax.ShapeDtypeStruct(x.shape, x.dtype),
        scratch_shapes=[
            pltpu.VMEM((CONV_HALO + ts, a), jnp.float32),
            pltpu.VMEM((POOL_HALO + ts, p_ch), jnp.float32),
            pltpu.VMEM((ts, p_ch), jnp.bfloat16),
            pltpu.VMEM((ts, a + p_ch), jnp.bfloat16),
        ],
        compiler_params=pltpu.CompilerParams(
            dimension_semantics=("arbitrary", "arbitrary"),
            vmem_limit_bytes=VMEM_LIMIT_BYTES),
        name="sandwich_layer",
    )(x, pre, win, cw, cb, wp, ps, wout, post)


def _pair_block_diag(w_pool):
    depth, groups, c, _ = w_pool.shape
    wp = w_pool.reshape(depth, groups // 2, 2, c, c)
    z = jnp.zeros_like(wp[:, :, 0])
    top = jnp.concatenate([wp[:, :, 0], z], axis=-1)
    bot = jnp.concatenate([z, wp[:, :, 1]], axis=-1)
    return jnp.concatenate([top, bot], axis=-2)


def kernel(x, pre_norm, w_in, conv_w, conv_b, w_pool, pool_scale, w_out, post_norm):
    depth = w_in.shape[0]
    win = w_in.astype(jnp.bfloat16)
    wout = w_out.astype(jnp.bfloat16)
    wp = _pair_block_diag(w_pool).astype(jnp.bfloat16)
    pre = pre_norm[:, None, :]
    cb = conv_b[:, None, :]
    ps = pool_scale[:, None, :]
    post = post_norm[:, None, :]
    for l in range(depth):
        x = _layer(l, x, pre, win, conv_w, cb, wp, ps, wout, post)
    return x
```

```python
import functools

import jax
import jax.numpy as jnp
from jax import lax
from jax.experimental import pallas as pl
from jax.experimental.pallas import tpu as pltpu

RMS_EPS = 1e-6
CONV_K = 3
POOL_WINDOWS = (2, 4, 8, 16)
N_POOL_GROUPS = len(POOL_WINDOWS)

SEQ_TILE = 1024
OUT_ROWS = 256
CONV_HALO = 8
POOL_HALO = 16
VMEM_LIMIT_BYTES = 58 * 1024 * 1024


def _rms_inv(v):
    return lax.rsqrt(jnp.mean(v * v, axis=-1, keepdims=True) + RMS_EPS)


def _silu(z):
    return z * jax.nn.sigmoid(z)


def _shift_rows(v, k):
    if k % 8 == 0:
        return jnp.concatenate([v[-k:], v[:-k]], axis=0)
    return pltpu.roll(v, k, axis=0)


def _mix_stage(j, x_ref, pre_ref, win_ref, cw_ref, cb_ref, wp_ref, ps_ref,
               vext_ref, uext_ref, p_ref, y_ref, *, ts, a, pool_gc, tiles_per_seq, first_step):
    p_ch = uext_ref.shape[1]
    if first_step:
        vext_ref[0:CONV_HALO, :] = jnp.zeros((CONV_HALO, a), jnp.float32)
        uext_ref[0:POOL_HALO, :] = jnp.zeros((POOL_HALO, p_ch), jnp.float32)
    carry_on = j + 1 < tiles_per_seq

    x = x_ref[0]
    xg = (x * pre_ref[...]).astype(jnp.bfloat16)

    def proj(lo, width):
        return jnp.dot(xg, win_ref[:, lo:lo + width], preferred_element_type=jnp.float32)

    u = proj(4 * a, p_ch)
    yield
    inv = _rms_inv(x)

    u = u * inv
    uext_ref[POOL_HALO:POOL_HALO + ts, :] = u
    row = lax.broadcasted_iota(jnp.int32, (POOL_HALO, pool_gc), 0)
    pos1 = (j * ts + row + 1).astype(jnp.float32)
    for g, w in enumerate(POOL_WINDOWS):
        lo = g * pool_gc
        ug = u[:, lo:lo + pool_gc]
        s = uext_ref[:, lo:lo + pool_gc]
        m = 1
        while m < w:
            s = s + _shift_rows(s, m)
            m *= 2
        s = s[POOL_HALO:]
        p_body = s * (1.0 / w) - ug
        p_head = s[0:POOL_HALO] / jnp.minimum(pos1, float(w)) - ug[0:POOL_HALO]
        p = jnp.concatenate([p_head, p_body[POOL_HALO:]], axis=0)
        p_ref[:, lo:lo + pool_gc] = p.astype(jnp.bfloat16)
    uext_ref[0:POOL_HALO, :] = jnp.where(carry_on, u[ts - POOL_HALO:], 0.0)

    c = proj(a, a)
    yield
    h = proj(2 * a, a)
    yield
    v = (c * h) * (inv * inv)
    vext_ref[CONV_HALO:CONV_HALO + ts, :] = v
    vx = vext_ref[...]
    conv = (cw_ref[0:1, :] * _shift_rows(vx, 2)[CONV_HALO:]
            + cw_ref[1:2, :] * _shift_rows(vx, 1)[CONV_HALO:]
            + cw_ref[2:3, :] * v) + cb_ref[...]
    vext_ref[0:CONV_HALO, :] = jnp.where(carry_on, v[ts - CONV_HALO:], 0.0)

    yb = []
    for hh in range(wp_ref.shape[0]):
        lo = hh * 2 * pool_gc
        yb.append(jnp.dot(p_ref[:, lo:lo + 2 * pool_gc], wp_ref[hh], preferred_element_type=jnp.float32))
    z_a = proj(3 * a, a)
    yield
    gated = conv * _silu(z_a * inv)
    z_b = proj(4 * a + p_ch, p_ch)
    yield
    gate_b = _silu(z_b * inv) * ps_ref[...]
    for hh in range(wp_ref.shape[0]):
        lo = hh * 2 * pool_gc
        y_ref[:, a + lo:a + lo + 2 * pool_gc] = (yb[hh] * gate_b[:, lo:lo + 2 * pool_gc]).astype(jnp.bfloat16)
    b = proj(0, a)
    yield
    y_ref[:, 0:a] = ((b * inv) * gated).astype(jnp.bfloat16)


def _out_stage(y_ref, xres_ref, wout_ref, post_ref, o_ref, *, ts):
    for r in range(0, ts, OUT_ROWS):
        out = jnp.dot(y_ref[r:r + OUT_ROWS, :], wout_ref[...], preferred_element_type=jnp.float32)
        yield
        o_ref[0, r:r + OUT_ROWS, :] = xres_ref[0, r:r + OUT_ROWS, :] + (out * _rms_inv(out)) * post_ref[...]


def _interleave(first, second):
    pending = [first, second]
    while pending:
        for gen in list(pending):
            try:
                next(gen)
            except StopIteration:
                pending.remove(gen)


def _layer_kernel(x_ref, xres_ref, pre_ref, win_ref, cw_ref, cb_ref, wp_ref, ps_ref, wout_ref,
                  post_ref, o_ref, vext_ref, uext_ref, p_ref, y_ref, *,
                  seq_tile, conv_ch, pool_gc, tiles_per_seq, n_tiles):
    s = pl.program_id(0)
    mix = functools.partial(
        _mix_stage, s % tiles_per_seq, x_ref, pre_ref, win_ref, cw_ref, cb_ref, wp_ref, ps_ref,
        vext_ref, uext_ref, p_ref, y_ref.at[s % 2], ts=seq_tile, a=conv_ch, pool_gc=pool_gc,
        tiles_per_seq=tiles_per_seq)
    out = functools.partial(
        _out_stage, y_ref.at[(s + 1) % 2], xres_ref, wout_ref, post_ref, o_ref, ts=seq_tile)

    @pl.when(s == 0)
    def _():
        _interleave(mix(first_step=True), iter(()))

    @pl.when(jnp.logical_and(s > 0, s < n_tiles))
    def _():
        _interleave(mix(first_step=False), out())

    @pl.when(s == n_tiles)
    def _():
        _interleave(out(), iter(()))


def _layer(l, x, pre, win, cw, cb, wp, ps, wout, post):
    bsz, seq, d = x.shape
    a = cw.shape[-1]
    p_ch = ps.shape[-1]
    pool_gc = wp.shape[-1] // 2
    in_cols = win.shape[-1]
    ts = SEQ_TILE
    assert seq % ts == 0 and ts % OUT_ROWS == 0
    assert in_cols == 4 * a + 2 * p_ch and p_ch == N_POOL_GROUPS * pool_gc
    tiles_per_seq = seq // ts
    n_tiles = bsz * tiles_per_seq

    def layer_block(*shape):
        return pl.BlockSpec((None,) + shape, lambda s: (l,) + (0,) * len(shape),
                            pipeline_mode=pl.Buffered(1))

    def tile_index(t):
        return (t // tiles_per_seq, t % tiles_per_seq, 0)

    mix_tile = lambda s: tile_index(jnp.minimum(s, n_tiles - 1))
    out_tile = lambda s: tile_index(jnp.maximum(s - 1, 0))

    kernel = functools.partial(_layer_kernel, seq_tile=ts, conv_ch=a, pool_gc=pool_gc,
                               tiles_per_seq=tiles_per_seq, n_tiles=n_tiles)
    return pl.pallas_call(
        kernel,
        grid=(n_tiles + 1,),
        in_specs=[
            pl.BlockSpec((1, ts, d), mix_tile),
            pl.BlockSpec((1, ts, d), out_tile),
            layer_block(1, d),
            layer_block(d, in_cols),
            layer_block(CONV_K, a),
            layer_block(1, a),
            layer_block(N_POOL_GROUPS // 2, 2 * pool_gc, 2 * pool_gc),
            layer_block(1, p_ch),
            layer_block(a + p_ch, d),
            layer_block(1, d),
        ],
        out_specs=pl.BlockSpec((1, ts, d), out_tile),
        out_shape=jax.ShapeDtypeStruct(x.shape, x.dtype),
        scratch_shapes=[
            pltpu.VMEM((CONV_HALO + ts, a), jnp.float32),
            pltpu.VMEM((POOL_HALO + ts, p_ch), jnp.float32),
            pltpu.VMEM((ts, p_ch), jnp.bfloat16),
            pltpu.VMEM((2, ts, a + p_ch), jnp.bfloat16),
        ],
        compiler_params=pltpu.CompilerParams(
            dimension_semantics=("arbitrary",),
            vmem_limit_bytes=VMEM_LIMIT_BYTES),
        name="sandwich_layer",
    )(x, x, pre, win, cw, cb, wp, ps, wout, post)


def _pair_block_diag(w_pool):
    depth, groups, c, _ = w_pool.shape
    wp = w_pool.reshape(depth, groups // 2, 2, c, c)
    z = jnp.zeros_like(wp[:, :, 0])
    top = jnp.concatenate([wp[:, :, 0], z], axis=-1)
    bot = jnp.concatenate([z, wp[:, :, 1]], axis=-1)
    return jnp.concatenate([top, bot], axis=-2)


def kernel(x, pre_norm, w_in, conv_w, conv_b, w_pool, pool_scale, w_out, post_norm):
    depth = w_in.shape[0]
    win = w_in.astype(jnp.bfloat16)
    wout = w_out.astype(jnp.bfloat16)
    wp = _pair_block_diag(w_pool).astype(jnp.bfloat16)
    pre = pre_norm[:, None, :]
    cb = conv_b[:, None, :]
    ps = pool_scale[:, None, :]
    post = post_norm[:, None, :]
    for l in range(depth):
        x = _layer(l, x, pre, win, conv_w, cb, wp, ps, wout, post)
    return x
```

```python
import functools

import jax
import jax.numpy as jnp
from jax import lax
from jax.experimental import pallas as pl
from jax.experimental.pallas import tpu as pltpu

RMS_EPS = 1e-6
CONV_K = 3
POOL_WINDOWS = (2, 4, 8, 16)
N_POOL_GROUPS = len(POOL_WINDOWS)

SEQ_TILE = 1024
OUT_ROWS = 256
CONV_HALO = 8
POOL_HALO = 16
VMEM_LIMIT_BYTES = 58 * 1024 * 1024


def _rms_inv(v):
    return lax.rsqrt(jnp.mean(v * v, axis=-1, keepdims=True) + RMS_EPS)


def _silu(z):
    return z * jax.nn.sigmoid(z)


def _shift_rows(v, k):
    if k % 8 == 0:
        return jnp.concatenate([v[-k:], v[:-k]], axis=0)
    return pltpu.roll(v, k, axis=0)


def _mix_stage(j, x_ref, pre_ref, win_ref, cw_ref, cb_ref, wp_ref, ps_ref,
               vext_ref, uext_ref, p_ref, y_ref, *, ts, a, pool_gc, tiles_per_seq, first_step):
    p_ch = uext_ref.shape[1]
    if first_step:
        vext_ref[0:CONV_HALO, :] = jnp.zeros((CONV_HALO, a), jnp.float32)
        uext_ref[0:POOL_HALO, :] = jnp.zeros((POOL_HALO, p_ch), jnp.float32)
    carry_on = j + 1 < tiles_per_seq

    x = x_ref[0]
    hn = ((x * _rms_inv(x)) * pre_ref[...]).astype(jnp.bfloat16)

    def proj(lo, width):
        return jnp.dot(hn, win_ref[:, lo:lo + width], preferred_element_type=jnp.float32)

    u = proj(4 * a, p_ch)
    yield
    uext_ref[POOL_HALO:POOL_HALO + ts, :] = u
    row = lax.broadcasted_iota(jnp.int32, (POOL_HALO, pool_gc), 0)
    pos1 = (j * ts + row + 1).astype(jnp.float32)
    for g, w in enumerate(POOL_WINDOWS):
        lo = g * pool_gc
        ug = u[:, lo:lo + pool_gc]
        s = uext_ref[:, lo:lo + pool_gc]
        m = 1
        while m < w:
            s = s + _shift_rows(s, m)
            m *= 2
        s = s[POOL_HALO:]
        p_body = s * (1.0 / w) - ug
        p_head = s[0:POOL_HALO] / jnp.minimum(pos1, float(w)) - ug[0:POOL_HALO]
        p = jnp.concatenate([p_head, p_body[POOL_HALO:]], axis=0)
        p_ref[:, lo:lo + pool_gc] = p.astype(jnp.bfloat16)
    uext_ref[0:POOL_HALO, :] = jnp.where(carry_on, u[ts - POOL_HALO:], 0.0)

    c = proj(a, a)
    yield
    h = proj(2 * a, a)
    yield
    v = c * h
    vext_ref[CONV_HALO:CONV_HALO + ts, :] = v
    vx = vext_ref[...]
    conv = (cw_ref[0:1, :] * _shift_rows(vx, 2)[CONV_HALO:]
            + cw_ref[1:2, :] * _shift_rows(vx, 1)[CONV_HALO:]
            + cw_ref[2:3, :] * v) + cb_ref[...]
    vext_ref[0:CONV_HALO, :] = jnp.where(carry_on, v[ts - CONV_HALO:], 0.0)

    yb = []
    for hh in range(wp_ref.shape[0]):
        lo = hh * 2 * pool_gc
        yb.append(jnp.dot(p_ref[:, lo:lo + 2 * pool_gc], wp_ref[hh], preferred_element_type=jnp.float32))
    z_a = proj(3 * a, a)
    yield
    gated = conv * _silu(z_a)
    z_b = proj(4 * a + p_ch, p_ch)
    yield
    gate_b = _silu(z_b) * ps_ref[...]
    for hh in range(wp_ref.shape[0]):
        lo = hh * 2 * pool_gc
        y_ref[:, a + lo:a + lo + 2 * pool_gc] = (yb[hh] * gate_b[:, lo:lo + 2 * pool_gc]).astype(jnp.bfloat16)
    b = proj(0, a)
    yield
    y_ref[:, 0:a] = (b * gated).astype(jnp.bfloat16)


def _out_stage(y_ref, xres_ref, wout_ref, post_ref, o_ref, *, ts):
    for r in range(0, ts, OUT_ROWS):
        out = jnp.dot(y_ref[r:r + OUT_ROWS, :], wout_ref[...], preferred_element_type=jnp.float32)
        yield
        o_ref[0, r:r + OUT_ROWS, :] = xres_ref[0, r:r + OUT_ROWS, :] + (out * _rms_inv(out)) * post_ref[...]


def _interleave(first, second):
    pending = [first, second]
    while pending:
        for gen in list(pending):
            try:
                next(gen)
            except StopIteration:
                pending.remove(gen)


def _layer_kernel(x_ref, xres_ref, pre_ref, win_ref, cw_ref, cb_ref, wp_ref, ps_ref, wout_ref,
                  post_ref, o_ref, vext_ref, uext_ref, p_ref, y_ref, *,
                  seq_tile, conv_ch, pool_gc, tiles_per_seq, n_tiles):
    s = pl.program_id(0)
    mix = functools.partial(
        _mix_stage, s % tiles_per_seq, x_ref, pre_ref, win_ref, cw_ref, cb_ref, wp_ref, ps_ref,
        vext_ref, uext_ref, p_ref, y_ref.at[s % 2], ts=seq_tile, a=conv_ch, pool_gc=pool_gc,
        tiles_per_seq=tiles_per_seq)
    out = functools.partial(
        _out_stage, y_ref.at[(s + 1) % 2], xres_ref, wout_ref, post_ref, o_ref, ts=seq_tile)

    @pl.when(s == 0)
    def _():
        _interleave(mix(first_step=True), iter(()))

    @pl.when(jnp.logical_and(s > 0, s < n_tiles))
    def _():
        _interleave(out(), mix(first_step=False))

    @pl.when(s == n_tiles)
    def _():
        _interleave(out(), iter(()))


def _layer(l, x, pre, win, cw, cb, wp, ps, wout, post):
    bsz, seq, d = x.shape
    a = cw.shape[-1]
    p_ch = ps.shape[-1]
    pool_gc = wp.shape[-1] // 2
    in_cols = win.shape[-1]
    ts = SEQ_TILE
    assert seq % ts == 0 and ts % OUT_ROWS == 0
    assert in_cols == 4 * a + 2 * p_ch and p_ch == N_POOL_GROUPS * pool_gc
    tiles_per_seq = seq // ts
    n_tiles = bsz * tiles_per_seq

    def layer_block(*shape):
        return pl.BlockSpec((None,) + shape, lambda s: (l,) + (0,) * len(shape),
                            pipeline_mode=pl.Buffered(1))

    def tile_index(t):
        return (t // tiles_per_seq, t % tiles_per_seq, 0)

    mix_tile = lambda s: tile_index(jnp.minimum(s, n_tiles - 1))
    out_tile = lambda s: tile_index(jnp.maximum(s - 1, 0))

    kernel = functools.partial(_layer_kernel, seq_tile=ts, conv_ch=a, pool_gc=pool_gc,
                               tiles_per_seq=tiles_per_seq, n_tiles=n_tiles)
    return pl.pallas_call(
        kernel,
        grid=(n_tiles + 1,),
        in_specs=[
            pl.BlockSpec((1, ts, d), mix_tile),
            pl.BlockSpec((1, ts, d), out_tile),
            layer_block(1, d),
            layer_block(d, in_cols),
            layer_block(CONV_K, a),
            layer_block(1, a),
            layer_block(N_POOL_GROUPS // 2, 2 * pool_gc, 2 * pool_gc),
            layer_block(1, p_ch),
            layer_block(a + p_ch, d),
            layer_block(1, d),
        ],
        out_specs=pl.BlockSpec((1, ts, d), out_tile),
        out_shape=jax.ShapeDtypeStruct(x.shape, x.dtype),
        scratch_shapes=[
            pltpu.VMEM((CONV_HALO + ts, a), jnp.float32),
            pltpu.VMEM((POOL_HALO + ts, p_ch), jnp.float32),
            pltpu.VMEM((ts, p_ch), jnp.bfloat16),
            pltpu.VMEM((2, ts, a + p_ch), jnp.bfloat16),
        ],
        compiler_params=pltpu.CompilerParams(
            dimension_semantics=("arbitrary",),
            vmem_limit_bytes=VMEM_LIMIT_BYTES),
        name="sandwich_layer",
    )(x, x, pre, win, cw, cb, wp, ps, wout, post)


def _pair_block_diag(w_pool):
    depth, groups, c, _ = w_pool.shape
    wp = w_pool.reshape(depth, groups // 2, 2, c, c)
    z = jnp.zeros_like(wp[:, :, 0])
    top = jnp.concatenate([wp[:, :, 0], z], axis=-1)
    bot = jnp.concatenate([z, wp[:, :, 1]], axis=-1)
    return jnp.concatenate([top, bot], axis=-2)


def kernel(x, pre_norm, w_in, conv_w, conv_b, w_pool, pool_scale, w_out, post_norm):
    depth = w_in.shape[0]
    win = w_in.astype(jnp.bfloat16)
    wout = w_out.astype(jnp.bfloat16)
    wp = _pair_block_diag(w_pool).astype(jnp.bfloat16)
    pre = pre_norm[:, None, :]
    cb = conv_b[:, None, :]
    ps = pool_scale[:, None, :]
    post = post_norm[:, None, :]
    for l in range(depth):
        x = _layer(l, x, pre, win, conv_w, cb, wp, ps, wout, post)
    return x
```

```python
import functools

import jax
import jax.numpy as jnp
from jax import lax
from jax.experimental import pallas as pl
from jax.experimental.pallas import tpu as pltpu

RMS_EPS = 1e-6
CONV_K = 3
POOL_WINDOWS = (2, 4, 8, 16)
N_POOL_GROUPS = len(POOL_WINDOWS)

SEQ_TILE = 1024
OUT_ROWS = 512
LANES = 128
CONV_HALO = 8
POOL_HALO = 16
POOL_PAD = 8
POOL_TMPS = 5
VMEM_LIMIT_BYTES = 58 * 1024 * 1024


def _rms_inv(v):
    return lax.rsqrt(jnp.mean(v * v, axis=-1, keepdims=True) + RMS_EPS)


def _silu(z):
    return z * jax.nn.sigmoid(z)


def _shifted_rows(ref, start, rows):
    return ref[pl.ds(start, rows, stride=1), :]


def _mix_stage(j, x_ref, pre_ref, win_ref, cw_ref, cb_ref, wp_ref, ps_ref,
               vext_ref, uext_ref, tmp_ref, p_ref, y_ref, *, ts, a, tiles_per_seq, first_step):
    p_ch = N_POOL_GROUPS * LANES
    body = POOL_PAD + POOL_HALO
    ext = POOL_HALO + ts
    if first_step:
        vext_ref[:, 0:CONV_HALO, :] = jnp.zeros((a // LANES, CONV_HALO, LANES), jnp.float32)
        uext_ref[:, 0:body, :] = jnp.zeros((N_POOL_GROUPS, body, LANES), jnp.float32)
        tmp_ref[:, 0:POOL_PAD, :] = jnp.zeros((tmp_ref.shape[0], POOL_PAD, LANES), jnp.float32)
    carry_on = j + 1 < tiles_per_seq

    x = x_ref[0]
    xg = (x * pre_ref[...]).astype(jnp.bfloat16)

    def proj(lo, width):
        return jnp.dot(xg, win_ref[:, lo:lo + width], preferred_element_type=jnp.float32)

    u = proj(4 * a, p_ch)
    yield
    inv = _rms_inv(x)
    u = u * inv
    row = lax.broadcasted_iota(jnp.int32, (POOL_HALO, LANES), 0)
    pos1 = (j * ts + row + 1).astype(jnp.float32)
    n_tmp = 0
    for g, w in enumerate(POOL_WINDOWS):
        lo = g * LANES
        ug = u[:, lo:lo + LANES]
        ue = uext_ref.at[g]
        ue[body:body + ts, :] = ug
        s = ue[POOL_PAD:POOL_PAD + ext, :] + _shifted_rows(ue, POOL_PAD - 1, ext)
        m = 2
        while m < w and m < 8:
            t = tmp_ref.at[n_tmp]
            n_tmp += 1
            t[POOL_PAD:POOL_PAD + ext, :] = s
            s = s + _shifted_rows(t, POOL_PAD - m, ext)
            m *= 2
        if m < w:
            s = s + jnp.concatenate([s[:8], s[:-8]], axis=0)
        s = s[POOL_HALO:]
        p_body = s * (1.0 / w) - ug
        p_head = s[0:POOL_HALO] / jnp.minimum(pos1, float(w)) - ug[0:POOL_HALO]
        p = jnp.concatenate([p_head, p_body[POOL_HALO:]], axis=0)
        p_ref[:, lo:lo + LANES] = p.astype(jnp.bfloat16)
        ue[POOL_PAD:body, :] = jnp.where(carry_on, ug[ts - POOL_HALO:], 0.0)

    c = proj(a, a)
    yield
    h = proj(2 * a, a)
    yield
    v = (c * h) * (inv * inv)
    conv = []
    for q in range(a // LANES):
        lo = q * LANES
        vq = v[:, lo:lo + LANES]
        ve = vext_ref.at[q]
        ve[CONV_HALO:CONV_HALO + ts, :] = vq
        conv.append((cw_ref[0:1, lo:lo + LANES] * _shifted_rows(ve, CONV_HALO - 2, ts)
                     + cw_ref[1:2, lo:lo + LANES] * _shifted_rows(ve, CONV_HALO - 1, ts)
                     + cw_ref[2:3, lo:lo + LANES] * vq) + cb_ref[:, lo:lo + LANES])
        ve[0:CONV_HALO, :] = jnp.where(carry_on, vq[ts - CONV_HALO:], 0.0)
    conv = jnp.concatenate(conv, axis=1)

    yb = []
    for hh in range(wp_ref.shape[0]):
        lo = hh * 2 * LANES
        yb.append(jnp.dot(p_ref[:, lo:lo + 2 * LANES], wp_ref[hh], preferred_element_type=jnp.float32))
    z_a = proj(3 * a, a)
    yield
    gated = conv * _silu(z_a * inv)
    z_b = proj(4 * a + p_ch, p_ch)
    yield
    gate_b = _silu(z_b * inv) * ps_ref[...]
    for hh in range(wp_ref.shape[0]):
        lo = hh * 2 * LANES
        y_ref[:, a + lo:a + lo + 2 * LANES] = (yb[hh] * gate_b[:, lo:lo + 2 * LANES]).astype(jnp.bfloat16)
    b = proj(0, a)
    yield
    y_ref[:, 0:a] = ((b * inv) * gated).astype(jnp.bfloat16)


def _out_stage(y_ref, xres_ref, wout_ref, post_ref, o_ref, *, ts):
    for r in range(0, ts, OUT_ROWS):
        out = jnp.dot(y_ref[r:r + OUT_ROWS, :], wout_ref[...], preferred_element_type=jnp.float32)
        yield
        o_ref[0, r:r + OUT_ROWS, :] = xres_ref[0, r:r + OUT_ROWS, :] + (out * _rms_inv(out)) * post_ref[...]


def _interleave(first, second):
    pending = [first, second]
    while pending:
        for gen in list(pending):
            try:
                next(gen)
            except StopIteration:
                pending.remove(gen)


def _layer_kernel(x_ref, xres_ref, pre_ref, win_ref, cw_ref, cb_ref, wp_ref, ps_ref, wout_ref,
                  post_ref, o_ref, vext_ref, uext_ref, tmp_ref, p_ref, y_ref, *,
                  seq_tile, conv_ch, tiles_per_seq, n_tiles):
    s = pl.program_id(0)
    mix = functools.partial(
        _mix_stage, s % tiles_per_seq, x_ref, pre_ref, win_ref, cw_ref, cb_ref, wp_ref, ps_ref,
        vext_ref, uext_ref, tmp_ref, p_ref, y_ref.at[s % 2], ts=seq_tile, a=conv_ch,
        tiles_per_seq=tiles_per_seq)
    out = functools.partial(
        _out_stage, y_ref.at[(s + 1) % 2], xres_ref, wout_ref, post_ref, o_ref, ts=seq_tile)

    @pl.when(s == 0)
    def _():
        _interleave(mix(first_step=True), iter(()))

    @pl.when(jnp.logical_and(s > 0, s < n_tiles))
    def _():
        _interleave(mix(first_step=False), out())

    @pl.when(s == n_tiles)
    def _():
        _interleave(out(), iter(()))


def _layer(l, x, pre, win, cw, cb, wp, ps, wout, post):
    bsz, seq, d = x.shape
    a = cw.shape[-1]
    p_ch = ps.shape[-1]
    pool_gc = wp.shape[-1] // 2
    in_cols = win.shape[-1]
    ts = SEQ_TILE
    assert seq % ts == 0 and ts % OUT_ROWS == 0
    assert in_cols == 4 * a + 2 * p_ch and p_ch == N_POOL_GROUPS * pool_gc
    assert pool_gc == LANES and a % LANES == 0
    tiles_per_seq = seq // ts
    n_tiles = bsz * tiles_per_seq

    def layer_block(*shape):
        return pl.BlockSpec((None,) + shape, lambda s: (l,) + (0,) * len(shape),
                            pipeline_mode=pl.Buffered(1))

    def tile_index(t):
        return (t // tiles_per_seq, t % tiles_per_seq, 0)

    mix_tile = lambda s: tile_index(jnp.minimum(s, n_tiles - 1))
    out_tile = lambda s: tile_index(jnp.maximum(s - 1, 0))

    kernel = functools.partial(_layer_kernel, seq_tile=ts, conv_ch=a,
                               tiles_per_seq=tiles_per_seq, n_tiles=n_tiles)
    return pl.pallas_call(
        kernel,
        grid=(n_tiles + 1,),
        in_specs=[
            pl.BlockSpec((1, ts, d), mix_tile),
            pl.BlockSpec((1, ts, d), out_tile),
            layer_block(1, d),
            layer_block(d, in_cols),
            layer_block(CONV_K, a),
            layer_block(1, a),
            layer_block(N_POOL_GROUPS // 2, 2 * pool_gc, 2 * pool_gc),
            layer_block(1, p_ch),
            layer_block(a + p_ch, d),
            layer_block(1, d),
        ],
        out_specs=pl.BlockSpec((1, ts, d), out_tile),
        out_shape=jax.ShapeDtypeStruct(x.shape, x.dtype),
        scratch_shapes=[
            pltpu.VMEM((a // LANES, CONV_HALO + ts, LANES), jnp.float32),
            pltpu.VMEM((N_POOL_GROUPS, POOL_PAD + POOL_HALO + ts, LANES), jnp.float32),
            pltpu.VMEM((POOL_TMPS, POOL_PAD + POOL_HALO + ts, LANES), jnp.float32),
            pltpu.VMEM((ts, p_ch), jnp.bfloat16),
            pltpu.VMEM((2, ts, a + p_ch), jnp.bfloat16),
        ],
        compiler_params=pltpu.CompilerParams(
            dimension_semantics=("arbitrary",),
            vmem_limit_bytes=VMEM_LIMIT_BYTES),
        name="sandwich_layer",
    )(x, x, pre, win, cw, cb, wp, ps, wout, post)


def _pair_block_diag(w_pool):
    depth, groups, c, _ = w_pool.shape
    wp = w_pool.reshape(depth, groups // 2, 2, c, c)
    z = jnp.zeros_like(wp[:, :, 0])
    top = jnp.concatenate([wp[:, :, 0], z], axis=-1)
    bot = jnp.concatenate([z, wp[:, :, 1]], axis=-1)
    return jnp.concatenate([top, bot], axis=-2)


def kernel(x, pre_norm, w_in, conv_w, conv_b, w_pool, pool_scale, w_out, post_norm):
    depth = w_in.shape[0]
    win = w_in.astype(jnp.bfloat16)
    wout = w_out.astype(jnp.bfloat16)
    wp = _pair_block_diag(w_pool).astype(jnp.bfloat16)
    pre = pre_norm[:, None, :]
    cb = conv_b[:, None, :]
    ps = pool_scale[:, None, :]
    post = post_norm[:, None, :]
    for l in range(depth):
        x = _layer(l, x, pre, win, conv_w, cb, wp, ps, wout, post)
    return x
```

```python
import functools

import jax
import jax.numpy as jnp
from jax import lax
from jax.experimental import pallas as pl
from jax.experimental.pallas import tpu as pltpu

RMS_EPS = 1e-6
CONV_K = 3
POOL_WINDOWS = (2, 4, 8, 16)
N_POOL_GROUPS = len(POOL_WINDOWS)

SEQ_TILE = 512
OUT_ROWS = 256
LANES = 128
CONV_HALO = 8
POOL_HALO = 16
POOL_PAD = 8
POOL_TMPS = 5
VMEM_LIMIT_BYTES = 58 * 1024 * 1024


def _rms_inv(v):
    return lax.rsqrt(jnp.mean(v * v, axis=-1, keepdims=True) + RMS_EPS)


def _silu(z):
    return z * jax.nn.sigmoid(z)


def _shifted_rows(ref, start, rows):
    return ref[pl.ds(start, rows, stride=1), :]


def _mix_stage(j, x_ref, pre_ref, win_ref, cw_ref, cb_ref, wp_ref, ps_ref,
               vext_ref, uext_ref, tmp_ref, p_ref, y_ref, *, ts, a, tiles_per_seq, first_step):
    p_ch = N_POOL_GROUPS * LANES
    body = POOL_PAD + POOL_HALO
    ext = POOL_HALO + ts
    if first_step:
        vext_ref[:, 0:CONV_HALO, :] = jnp.zeros((a // LANES, CONV_HALO, LANES), jnp.float32)
        uext_ref[:, 0:body, :] = jnp.zeros((N_POOL_GROUPS, body, LANES), jnp.float32)
        tmp_ref[:, 0:POOL_PAD, :] = jnp.zeros((tmp_ref.shape[0], POOL_PAD, LANES), jnp.float32)
    carry_on = j + 1 < tiles_per_seq

    x = x_ref[0]
    xg = (x * pre_ref[...]).astype(jnp.bfloat16)

    def proj(lo, width):
        return jnp.dot(xg, win_ref[:, lo:lo + width], preferred_element_type=jnp.float32)

    u = proj(4 * a, p_ch)
    yield
    inv = _rms_inv(x)
    u = u * inv
    row = lax.broadcasted_iota(jnp.int32, (POOL_HALO, LANES), 0)
    pos1 = (j * ts + row + 1).astype(jnp.float32)
    n_tmp = 0
    for g, w in enumerate(POOL_WINDOWS):
        lo = g * LANES
        ug = u[:, lo:lo + LANES]
        ue = uext_ref.at[g]
        ue[body:body + ts, :] = ug
        s = ue[POOL_PAD:POOL_PAD + ext, :] + _shifted_rows(ue, POOL_PAD - 1, ext)
        m = 2
        while m < w and m < 8:
            t = tmp_ref.at[n_tmp]
            n_tmp += 1
            t[POOL_PAD:POOL_PAD + ext, :] = s
            s = s + _shifted_rows(t, POOL_PAD - m, ext)
            m *= 2
        if m < w:
            s = s + jnp.concatenate([s[:8], s[:-8]], axis=0)
        s = s[POOL_HALO:]
        p_body = s * (1.0 / w) - ug
        p_head = s[0:POOL_HALO] / jnp.minimum(pos1, float(w)) - ug[0:POOL_HALO]
        p = jnp.concatenate([p_head, p_body[POOL_HALO:]], axis=0)
        p_ref[:, lo:lo + LANES] = p.astype(jnp.bfloat16)
        ue[POOL_PAD:body, :] = jnp.where(carry_on, ug[ts - POOL_HALO:], 0.0)

    c = proj(a, a)
    yield
    h = proj(2 * a, a)
    yield
    v = (c * h) * (inv * inv)
    conv = []
    for q in range(a // LANES):
        lo = q * LANES
        vq = v[:, lo:lo + LANES]
        ve = vext_ref.at[q]
        ve[CONV_HALO:CONV_HALO + ts, :] = vq
        conv.append((cw_ref[0:1, lo:lo + LANES] * _shifted_rows(ve, CONV_HALO - 2, ts)
                     + cw_ref[1:2, lo:lo + LANES] * _shifted_rows(ve, CONV_HALO - 1, ts)
                     + cw_ref[2:3, lo:lo + LANES] * vq) + cb_ref[:, lo:lo + LANES])
        ve[0:CONV_HALO, :] = jnp.where(carry_on, vq[ts - CONV_HALO:], 0.0)
    conv = jnp.concatenate(conv, axis=1)

    yb = []
    for hh in range(wp_ref.shape[0]):
        lo = hh * 2 * LANES
        yb.append(jnp.dot(p_ref[:, lo:lo + 2 * LANES], wp_ref[hh], preferred_element_type=jnp.float32))
    z_a = proj(3 * a, a)
    yield
    gated = conv * _silu(z_a * inv)
    z_b = proj(4 * a + p_ch, p_ch)
    yield
    gate_b = _silu(z_b * inv) * ps_ref[...]
    for hh in range(wp_ref.shape[0]):
        lo = hh * 2 * LANES
        y_ref[:, a + lo:a + lo + 2 * LANES] = (yb[hh] * gate_b[:, lo:lo + 2 * LANES]).astype(jnp.bfloat16)
    b = proj(0, a)
    yield
    y_ref[:, 0:a] = ((b * inv) * gated).astype(jnp.bfloat16)


def _out_stage(y_ref, xres_ref, wout_ref, post_ref, o_ref, *, ts):
    for r in range(0, ts, OUT_ROWS):
        out = jnp.dot(y_ref[r:r + OUT_ROWS, :], wout_ref[...], preferred_element_type=jnp.float32)
        yield
        o_ref[0, r:r + OUT_ROWS, :] = xres_ref[0, r:r + OUT_ROWS, :] + (out * _rms_inv(out)) * post_ref[...]


def _interleave(first, second):
    pending = [first, second]
    while pending:
        for gen in list(pending):
            try:
                next(gen)
            except StopIteration:
                pending.remove(gen)


def _layer_kernel(x_ref, xres_ref, pre_ref, win_ref, cw_ref, cb_ref, wp_ref, ps_ref, wout_ref,
                  post_ref, o_ref, vext_ref, uext_ref, tmp_ref, p_ref, y_ref, *,
                  seq_tile, conv_ch, tiles_per_seq, n_tiles):
    s = pl.program_id(0)
    mix = functools.partial(
        _mix_stage, s % tiles_per_seq, x_ref, pre_ref, win_ref, cw_ref, cb_ref, wp_ref, ps_ref,
        vext_ref, uext_ref, tmp_ref, p_ref, y_ref.at[s % 2], ts=seq_tile, a=conv_ch,
        tiles_per_seq=tiles_per_seq)
    out = functools.partial(
        _out_stage, y_ref.at[(s + 1) % 2], xres_ref, wout_ref, post_ref, o_ref, ts=seq_tile)

    @pl.when(s == 0)
    def _():
        _interleave(mix(first_step=True), iter(()))

    @pl.when(jnp.logical_and(s > 0, s < n_tiles))
    def _():
        _interleave(mix(first_step=False), out())

    @pl.when(s == n_tiles)
    def _():
        _interleave(out(), iter(()))


def _layer(l, x, pre, win, cw, cb, wp, ps, wout, post):
    bsz, seq, d = x.shape
    a = cw.shape[-1]
    p_ch = ps.shape[-1]
    pool_gc = wp.shape[-1] // 2
    in_cols = win.shape[-1]
    ts = SEQ_TILE
    assert seq % ts == 0 and ts % OUT_ROWS == 0
    assert in_cols == 4 * a + 2 * p_ch and p_ch == N_POOL_GROUPS * pool_gc
    assert pool_gc == LANES and a % LANES == 0
    tiles_per_seq = seq // ts
    n_tiles = bsz * tiles_per_seq

    def layer_block(*shape):
        return pl.BlockSpec((None,) + shape, lambda s: (l,) + (0,) * len(shape),
                            pipeline_mode=pl.Buffered(1))

    def tile_index(t):
        return (t // tiles_per_seq, t % tiles_per_seq, 0)

    mix_tile = lambda s: tile_index(jnp.minimum(s, n_tiles - 1))
    out_tile = lambda s: tile_index(jnp.maximum(s - 1, 0))

    kernel = functools.partial(_layer_kernel, seq_tile=ts, conv_ch=a,
                               tiles_per_seq=tiles_per_seq, n_tiles=n_tiles)
    return pl.pallas_call(
        kernel,
        grid=(n_tiles + 1,),
        in_specs=[
            pl.BlockSpec((1, ts, d), mix_tile),
            pl.BlockSpec((1, ts, d), out_tile),
            layer_block(1, d),
            layer_block(d, in_cols),
            layer_block(CONV_K, a),
            layer_block(1, a),
            layer_block(N_POOL_GROUPS // 2, 2 * pool_gc, 2 * pool_gc),
            layer_block(1, p_ch),
            layer_block(a + p_ch, d),
            layer_block(1, d),
        ],
        out_specs=pl.BlockSpec((1, ts, d), out_tile),
        out_shape=jax.ShapeDtypeStruct(x.shape, x.dtype),
        scratch_shapes=[
            pltpu.VMEM((a // LANES, CONV_HALO + ts, LANES), jnp.float32),
            pltpu.VMEM((N_POOL_GROUPS, POOL_PAD + POOL_HALO + ts, LANES), jnp.float32),
            pltpu.VMEM((POOL_TMPS, POOL_PAD + POOL_HALO + ts, LANES), jnp.float32),
            pltpu.VMEM((ts, p_ch), jnp.bfloat16),
            pltpu.VMEM((2, ts, a + p_ch), jnp.bfloat16),
        ],
        compiler_params=pltpu.CompilerParams(
            dimension_semantics=("arbitrary",),
            vmem_limit_bytes=VMEM_LIMIT_BYTES),
        name="sandwich_layer",
    )(x, x, pre, win, cw, cb, wp, ps, wout, post)


def _pair_block_diag(w_pool):
    depth, groups, c, _ = w_pool.shape
    wp = w_pool.reshape(depth, groups // 2, 2, c, c)
    z = jnp.zeros_like(wp[:, :, 0])
    top = jnp.concatenate([wp[:, :, 0], z], axis=-1)
    bot = jnp.concatenate([z, wp[:, :, 1]], axis=-1)
    return jnp.concatenate([top, bot], axis=-2)


def kernel(x, pre_norm, w_in, conv_w, conv_b, w_pool, pool_scale, w_out, post_norm):
    depth = w_in.shape[0]
    win = w_in.astype(jnp.bfloat16)
    wout = w_out.astype(jnp.bfloat16)
    wp = _pair_block_diag(w_pool).astype(jnp.bfloat16)
    pre = pre_norm[:, None, :]
    cb = conv_b[:, None, :]
    ps = pool_scale[:, None, :]
    post = post_norm[:, None, :]
    for l in range(depth):
        x = _layer(l, x, pre, win, conv_w, cb, wp, ps, wout, post)
    return x
```

```python
import functools

import jax
import jax.numpy as jnp
from jax import lax
from jax.experimental import pallas as pl
from jax.experimental.pallas import tpu as pltpu

RMS_EPS = 1e-6
CONV_K = 3
POOL_WINDOWS = (2, 4, 8, 16)
N_POOL_GROUPS = len(POOL_WINDOWS)

SEQ_TILE = 512
OUT_ROWS = 256
LANES = 128
CONV_HALO = 8
POOL_HALO = 16
POOL_PAD = 8
POOL_TMPS = 5
VMEM_LIMIT_BYTES = 58 * 1024 * 1024


def _rms_inv(v):
    return lax.rsqrt(jnp.mean(v * v, axis=-1, keepdims=True) + RMS_EPS)


def _silu(z):
    return z * jax.nn.sigmoid(z)


def _shifted_rows(ref, start, rows):
    return ref[pl.ds(start, rows, stride=1), :]


ST_U, ST_C, ST_H, ST_ZA, ST_ZB, ST_B = range(6)
N_STAGE = 6


def _mix_stage(j, x_ref, pre_ref, win_ref, cw_ref, cb_ref, wp_ref, ps_ref,
               vext_ref, uext_ref, tmp_ref, stage_ref, yb_ref, p_ref, y_ref, *,
               ts, a, tiles_per_seq, first_step):
    p_ch = N_POOL_GROUPS * LANES
    body = POOL_PAD + POOL_HALO
    ext = POOL_HALO + ts
    if first_step:
        vext_ref[:, 0:CONV_HALO, :] = jnp.zeros((a // LANES, CONV_HALO, LANES), jnp.float32)
        uext_ref[:, 0:body, :] = jnp.zeros((N_POOL_GROUPS, body, LANES), jnp.float32)
        tmp_ref[:, 0:POOL_PAD, :] = jnp.zeros((tmp_ref.shape[0], POOL_PAD, LANES), jnp.float32)
    carry_on = j + 1 < tiles_per_seq

    x = x_ref[0]
    xg = (x * pre_ref[...]).astype(jnp.bfloat16)

    def proj_to(slot, lo):
        stage_ref[slot] = jnp.dot(xg, win_ref[:, lo:lo + a], preferred_element_type=jnp.float32)

    proj_to(ST_U, 4 * a)
    yield
    inv = _rms_inv(x)
    row = lax.broadcasted_iota(jnp.int32, (POOL_HALO, LANES), 0)
    pos1 = (j * ts + row + 1).astype(jnp.float32)
    n_tmp = 0
    for g, w in enumerate(POOL_WINDOWS):
        lo = g * LANES
        ug = stage_ref[ST_U, :, lo:lo + LANES] * inv
        ue = uext_ref.at[g]
        ue[body:body + ts, :] = ug
        s = ue[POOL_PAD:POOL_PAD + ext, :] + _shifted_rows(ue, POOL_PAD - 1, ext)
        m = 2
        while m < w and m < 8:
            t = tmp_ref.at[n_tmp]
            n_tmp += 1
            t[POOL_PAD:POOL_PAD + ext, :] = s
            s = s + _shifted_rows(t, POOL_PAD - m, ext)
            m *= 2
        if m < w:
            s = s + jnp.concatenate([s[:8], s[:-8]], axis=0)
        s = s[POOL_HALO:]
        p_body = s * (1.0 / w) - ug
        p_head = s[0:POOL_HALO] / jnp.minimum(pos1, float(w)) - ug[0:POOL_HALO]
        p = jnp.concatenate([p_head, p_body[POOL_HALO:]], axis=0)
        p_ref[:, lo:lo + LANES] = p.astype(jnp.bfloat16)
        ue[POOL_PAD:body, :] = jnp.where(carry_on, ug[ts - POOL_HALO:], 0.0)

    proj_to(ST_C, a)
    yield
    proj_to(ST_H, 2 * a)
    yield
    inv2 = inv * inv
    conv = []
    for q in range(a // LANES):
        lo = q * LANES
        vq = (stage_ref[ST_C, :, lo:lo + LANES] * stage_ref[ST_H, :, lo:lo + LANES]) * inv2
        ve = vext_ref.at[q]
        ve[CONV_HALO:CONV_HALO + ts, :] = vq
        conv.append((cw_ref[0:1, lo:lo + LANES] * _shifted_rows(ve, CONV_HALO - 2, ts)
                     + cw_ref[1:2, lo:lo + LANES] * _shifted_rows(ve, CONV_HALO - 1, ts)
                     + cw_ref[2:3, lo:lo + LANES] * vq) + cb_ref[:, lo:lo + LANES])
        ve[0:CONV_HALO, :] = jnp.where(carry_on, vq[ts - CONV_HALO:], 0.0)
    conv = jnp.concatenate(conv, axis=1)
    proj_to(ST_ZA, 3 * a)
    yield
    gated = conv * _silu(stage_ref[ST_ZA] * inv)
    proj_to(ST_ZB, 4 * a + p_ch)
    yield
    gate_b = _silu(stage_ref[ST_ZB] * inv) * ps_ref[...]
    proj_to(ST_B, 0)
    yield
    y_ref[:, 0:a] = ((stage_ref[ST_B] * inv) * gated).astype(jnp.bfloat16)

    for hh in range(wp_ref.shape[0]):
        lo = hh * 2 * LANES
        yb_ref[:, lo:lo + 2 * LANES] = jnp.dot(p_ref[:, lo:lo + 2 * LANES], wp_ref[hh],
                                               preferred_element_type=jnp.float32)
    yield
    y_ref[:, a:a + p_ch] = (yb_ref[...] * gate_b).astype(jnp.bfloat16)


def _out_stage(y_ref, xres_ref, wout_ref, post_ref, o_ref, so_ref, *, ts):
    for r in range(0, ts, OUT_ROWS):
        so_ref[r:r + OUT_ROWS, :] = jnp.dot(y_ref[r:r + OUT_ROWS, :], wout_ref[...],
                                            preferred_element_type=jnp.float32)
        yield
        out = so_ref[r:r + OUT_ROWS, :]
        o_ref[0, r:r + OUT_ROWS, :] = xres_ref[0, r:r + OUT_ROWS, :] + (out * _rms_inv(out)) * post_ref[...]


def _interleave(first, second):
    pending = [first, second]
    while pending:
        for gen in list(pending):
            try:
                next(gen)
            except StopIteration:
                pending.remove(gen)


def _layer_kernel(x_ref, xres_ref, pre_ref, win_ref, cw_ref, cb_ref, wp_ref, ps_ref, wout_ref,
                  post_ref, o_ref, vext_ref, uext_ref, tmp_ref, stage_ref, yb_ref, so_ref, p_ref, y_ref, *,
                  seq_tile, conv_ch, tiles_per_seq, n_tiles):
    s = pl.program_id(0)
    mix = functools.partial(
        _mix_stage, s % tiles_per_seq, x_ref, pre_ref, win_ref, cw_ref, cb_ref, wp_ref, ps_ref,
        vext_ref, uext_ref, tmp_ref, stage_ref, yb_ref, p_ref, y_ref.at[s % 2], ts=seq_tile, a=conv_ch,
        tiles_per_seq=tiles_per_seq)
    out = functools.partial(
        _out_stage, y_ref.at[(s + 1) % 2], xres_ref, wout_ref, post_ref, o_ref, so_ref, ts=seq_tile)

    @pl.when(s == 0)
    def _():
        _interleave(mix(first_step=True), iter(()))

    @pl.when(jnp.logical_and(s > 0, s < n_tiles))
    def _():
        _interleave(mix(first_step=False), out())

    @pl.when(s == n_tiles)
    def _():
        _interleave(out(), iter(()))


def _layer(l, x, pre, win, cw, cb, wp, ps, wout, post):
    bsz, seq, d = x.shape
    a = cw.shape[-1]
    p_ch = ps.shape[-1]
    pool_gc = wp.shape[-1] // 2
    in_cols = win.shape[-1]
    ts = SEQ_TILE
    assert seq % ts == 0 and ts % OUT_ROWS == 0
    assert in_cols == 4 * a + 2 * p_ch and p_ch == N_POOL_GROUPS * pool_gc
    assert pool_gc == LANES and a % LANES == 0 and p_ch == a
    tiles_per_seq = seq // ts
    n_tiles = bsz * tiles_per_seq

    def layer_block(*shape):
        return pl.BlockSpec((None,) + shape, lambda s: (l,) + (0,) * len(shape),
                            pipeline_mode=pl.Buffered(1))

    def tile_index(t):
        return (t // tiles_per_seq, t % tiles_per_seq, 0)

    mix_tile = lambda s: tile_index(jnp.minimum(s, n_tiles - 1))
    out_tile = lambda s: tile_index(jnp.maximum(s - 1, 0))

    kernel = functools.partial(_layer_kernel, seq_tile=ts, conv_ch=a,
                               tiles_per_seq=tiles_per_seq, n_tiles=n_tiles)
    return pl.pallas_call(
        kernel,
        grid=(n_tiles + 1,),
        in_specs=[
            pl.BlockSpec((1, ts, d), mix_tile),
            pl.BlockSpec((1, ts, d), out_tile),
            layer_block(1, d),
            layer_block(d, in_cols),
            layer_block(CONV_K, a),
            layer_block(1, a),
            layer_block(N_POOL_GROUPS // 2, 2 * pool_gc, 2 * pool_gc),
            layer_block(1, p_ch),
            layer_block(a + p_ch, d),
            layer_block(1, d),
        ],
        out_specs=pl.BlockSpec((1, ts, d), out_tile),
        out_shape=jax.ShapeDtypeStruct(x.shape, x.dtype),
        scratch_shapes=[
            pltpu.VMEM((a // LANES, CONV_HALO + ts, LANES), jnp.float32),
            pltpu.VMEM((N_POOL_GROUPS, POOL_PAD + POOL_HALO + ts, LANES), jnp.float32),
            pltpu.VMEM((POOL_TMPS, POOL_PAD + POOL_HALO + ts, LANES), jnp.float32),
            pltpu.VMEM((N_STAGE, ts, a), jnp.float32),
            pltpu.VMEM((ts, p_ch), jnp.float32),
            pltpu.VMEM((ts, d), jnp.float32),
            pltpu.VMEM((ts, p_ch), jnp.bfloat16),
            pltpu.VMEM((2, ts, a + p_ch), jnp.bfloat16),
        ],
        compiler_params=pltpu.CompilerParams(
            dimension_semantics=("arbitrary",),
            vmem_limit_bytes=VMEM_LIMIT_BYTES),
        name="sandwich_layer",
    )(x, x, pre, win, cw, cb, wp, ps, wout, post)


def _pair_block_diag(w_pool):
    depth, groups, c, _ = w_pool.shape
    wp = w_pool.reshape(depth, groups // 2, 2, c, c)
    z = jnp.zeros_like(wp[:, :, 0])
    top = jnp.concatenate([wp[:, :, 0], z], axis=-1)
    bot = jnp.concatenate([z, wp[:, :, 1]], axis=-1)
    return jnp.concatenate([top, bot], axis=-2)


def kernel(x, pre_norm, w_in, conv_w, conv_b, w_pool, pool_scale, w_out, post_norm):
    depth = w_in.shape[0]
    win = w_in.astype(jnp.bfloat16)
    wout = w_out.astype(jnp.bfloat16)
    wp = _pair_block_diag(w_pool).astype(jnp.bfloat16)
    pre = pre_norm[:, None, :]
    cb = conv_b[:, None, :]
    ps = pool_scale[:, None, :]
    post = post_norm[:, None, :]
    for l in range(depth):
        x = _layer(l, x, pre, win, conv_w, cb, wp, ps, wout, post)
    return x
```

```python
import functools

import jax
import jax.numpy as jnp
from jax import lax
from jax.experimental import pallas as pl
from jax.experimental.pallas import tpu as pltpu

RMS_EPS = 1e-6
CONV_K = 3
POOL_WINDOWS = (2, 4, 8, 16)
N_POOL_GROUPS = len(POOL_WINDOWS)

SEQ_TILE = 512
OUT_ROWS = 256
LANES = 128
CONV_HALO = 8
POOL_HALO = 16
POOL_PAD = 8
POOL_TMPS = 5
VMEM_LIMIT_BYTES = 58 * 1024 * 1024


def _rms_inv(v):
    return lax.rsqrt(jnp.mean(v * v, axis=-1, keepdims=True) + RMS_EPS)


def _silu_of_twice(h):
    return h * jnp.tanh(h) + h


def _shifted_rows(ref, start, rows):
    return ref[pl.ds(start, rows, stride=1), :]


ST_U, ST_C, ST_H, ST_ZA, ST_ZB, ST_B = range(6)
N_STAGE = 6


def _mix_stage(j, x_ref, pre_ref, win_ref, cw_ref, cb_ref, wp_ref, ps_ref,
               vext_ref, uext_ref, tmp_ref, stage_ref, yb_ref, p_ref, y_ref, *,
               ts, a, tiles_per_seq, first_step):
    p_ch = N_POOL_GROUPS * LANES
    body = POOL_PAD + POOL_HALO
    ext = POOL_HALO + ts
    if first_step:
        vext_ref[:, 0:CONV_HALO, :] = jnp.zeros((a // LANES, CONV_HALO, LANES), jnp.float32)
        uext_ref[:, 0:body, :] = jnp.zeros((N_POOL_GROUPS, body, LANES), jnp.float32)
        tmp_ref[:, 0:POOL_PAD, :] = jnp.zeros((tmp_ref.shape[0], POOL_PAD, LANES), jnp.float32)
    carry_on = j + 1 < tiles_per_seq

    x = x_ref[0]
    xg = (x * pre_ref[...]).astype(jnp.bfloat16)

    def proj_to(slot, lo):
        stage_ref[slot] = jnp.dot(xg, win_ref[:, lo:lo + a], preferred_element_type=jnp.float32)

    proj_to(ST_U, 4 * a)
    yield
    inv = _rms_inv(x)
    row = lax.broadcasted_iota(jnp.int32, (POOL_HALO, LANES), 0)
    pos1 = (j * ts + row + 1).astype(jnp.float32)
    n_tmp = 0
    for g, w in enumerate(POOL_WINDOWS):
        lo = g * LANES
        ug = stage_ref[ST_U, :, lo:lo + LANES] * inv
        ue = uext_ref.at[g]
        ue[body:body + ts, :] = ug
        s = ue[POOL_PAD:POOL_PAD + ext, :] + _shifted_rows(ue, POOL_PAD - 1, ext)
        m = 2
        while m < w and m < 8:
            t = tmp_ref.at[n_tmp]
            n_tmp += 1
            t[POOL_PAD:POOL_PAD + ext, :] = s
            s = s + _shifted_rows(t, POOL_PAD - m, ext)
            m *= 2
        if m < w:
            s = s + jnp.concatenate([s[:8], s[:-8]], axis=0)
        s = s[POOL_HALO:]
        p_body = s * (1.0 / w) - ug
        p_head = s[0:POOL_HALO] / jnp.minimum(pos1, float(w)) - ug[0:POOL_HALO]
        p = jnp.concatenate([p_head, p_body[POOL_HALO:]], axis=0)
        p_ref[:, lo:lo + LANES] = p.astype(jnp.bfloat16)
        ue[POOL_PAD:body, :] = jnp.where(carry_on, ug[ts - POOL_HALO:], 0.0)

    proj_to(ST_C, a)
    yield
    proj_to(ST_H, 2 * a)
    yield
    inv2 = inv * inv
    half_inv = 0.5 * inv
    conv = []
    for q in range(a // LANES):
        lo = q * LANES
        vq = (stage_ref[ST_C, :, lo:lo + LANES] * stage_ref[ST_H, :, lo:lo + LANES]) * inv2
        ve = vext_ref.at[q]
        ve[CONV_HALO:CONV_HALO + ts, :] = vq
        conv.append((cw_ref[0:1, lo:lo + LANES] * _shifted_rows(ve, CONV_HALO - 2, ts)
                     + cw_ref[1:2, lo:lo + LANES] * _shifted_rows(ve, CONV_HALO - 1, ts)
                     + cw_ref[2:3, lo:lo + LANES] * vq) + cb_ref[:, lo:lo + LANES])
        ve[0:CONV_HALO, :] = jnp.where(carry_on, vq[ts - CONV_HALO:], 0.0)
    conv = jnp.concatenate(conv, axis=1)
    proj_to(ST_ZA, 3 * a)
    yield
    stage_ref[ST_ZA] = conv * _silu_of_twice(stage_ref[ST_ZA] * half_inv)
    proj_to(ST_ZB, 4 * a + p_ch)
    yield
    stage_ref[ST_ZB] = _silu_of_twice(stage_ref[ST_ZB] * half_inv) * ps_ref[...]
    proj_to(ST_B, 0)
    yield
    y_ref[:, 0:a] = ((stage_ref[ST_B] * inv) * stage_ref[ST_ZA]).astype(jnp.bfloat16)

    for hh in range(wp_ref.shape[0]):
        lo = hh * 2 * LANES
        yb_ref[:, lo:lo + 2 * LANES] = jnp.dot(p_ref[:, lo:lo + 2 * LANES], wp_ref[hh],
                                               preferred_element_type=jnp.float32)
    yield
    y_ref[:, a:a + p_ch] = (yb_ref[...] * stage_ref[ST_ZB]).astype(jnp.bfloat16)


def _out_stage(y_ref, xres_ref, wout_ref, post_ref, o_ref, so_ref, *, ts):
    for r in range(0, ts, OUT_ROWS):
        so_ref[r:r + OUT_ROWS, :] = jnp.dot(y_ref[r:r + OUT_ROWS, :], wout_ref[...],
                                            preferred_element_type=jnp.float32)
        yield
        out = so_ref[r:r + OUT_ROWS, :]
        o_ref[0, r:r + OUT_ROWS, :] = xres_ref[0, r:r + OUT_ROWS, :] + (out * _rms_inv(out)) * post_ref[...]


def _interleave(first, second):
    pending = [first, second]
    while pending:
        for gen in list(pending):
            try:
                next(gen)
            except StopIteration:
                pending.remove(gen)


def _layer_kernel(x_ref, xres_ref, pre_ref, win_ref, cw_ref, cb_ref, wp_ref, ps_ref, wout_ref,
                  post_ref, o_ref, vext_ref, uext_ref, tmp_ref, stage_ref, yb_ref, so_ref, p_ref, y_ref, *,
                  seq_tile, conv_ch, tiles_per_seq, n_tiles):
    s = pl.program_id(0)
    mix = functools.partial(
        _mix_stage, s % tiles_per_seq, x_ref, pre_ref, win_ref, cw_ref, cb_ref, wp_ref, ps_ref,
        vext_ref, uext_ref, tmp_ref, stage_ref, yb_ref, p_ref, y_ref.at[s % 2], ts=seq_tile, a=conv_ch,
        tiles_per_seq=tiles_per_seq)
    out = functools.partial(
        _out_stage, y_ref.at[(s + 1) % 2], xres_ref, wout_ref, post_ref, o_ref, so_ref, ts=seq_tile)

    @pl.when(s == 0)
    def _():
        _interleave(mix(first_step=True), iter(()))

    @pl.when(jnp.logical_and(s > 0, s < n_tiles))
    def _():
        _interleave(mix(first_step=False), out())

    @pl.when(s == n_tiles)
    def _():
        _interleave(out(), iter(()))


def _layer(l, x, pre, win, cw, cb, wp, ps, wout, post):
    bsz, seq, d = x.shape
    a = cw.shape[-1]
    p_ch = ps.shape[-1]
    pool_gc = wp.shape[-1] // 2
    in_cols = win.shape[-1]
    ts = SEQ_TILE
    assert seq % ts == 0 and ts % OUT_ROWS == 0
    assert in_cols == 4 * a + 2 * p_ch and p_ch == N_POOL_GROUPS * pool_gc
    assert pool_gc == LANES and a % LANES == 0 and p_ch == a
    tiles_per_seq = seq // ts
    n_tiles = bsz * tiles_per_seq

    def layer_block(*shape):
        return pl.BlockSpec((None,) + shape, lambda s: (l,) + (0,) * len(shape),
                            pipeline_mode=pl.Buffered(1))

    def tile_index(t):
        return (t // tiles_per_seq, t % tiles_per_seq, 0)

    mix_tile = lambda s: tile_index(jnp.minimum(s, n_tiles - 1))
    out_tile = lambda s: tile_index(jnp.maximum(s - 1, 0))

    kernel = functools.partial(_layer_kernel, seq_tile=ts, conv_ch=a,
                               tiles_per_seq=tiles_per_seq, n_tiles=n_tiles)
    return pl.pallas_call(
        kernel,
        grid=(n_tiles + 1,),
        in_specs=[
            pl.BlockSpec((1, ts, d), mix_tile),
            pl.BlockSpec((1, ts, d), out_tile),
            layer_block(1, d),
            layer_block(d, in_cols),
            layer_block(CONV_K, a),
            layer_block(1, a),
            layer_block(N_POOL_GROUPS // 2, 2 * pool_gc, 2 * pool_gc),
            layer_block(1, p_ch),
            layer_block(a + p_ch, d),
            layer_block(1, d),
        ],
        out_specs=pl.BlockSpec((1, ts, d), out_tile),
        out_shape=jax.ShapeDtypeStruct(x.shape, x.dtype),
        scratch_shapes=[
            pltpu.VMEM((a // LANES, CONV_HALO + ts, LANES), jnp.float32),
            pltpu.VMEM((N_POOL_GROUPS, POOL_PAD + POOL_HALO + ts, LANES), jnp.float32),
            pltpu.VMEM((POOL_TMPS, POOL_PAD + POOL_HALO + ts, LANES), jnp.float32),
            pltpu.VMEM((N_STAGE, ts, a), jnp.float32),
            pltpu.VMEM((ts, p_ch), jnp.float32),
            pltpu.VMEM((ts, d), jnp.float32),
            pltpu.VMEM((ts, p_ch), jnp.bfloat16),
            pltpu.VMEM((2, ts, a + p_ch), jnp.bfloat16),
        ],
        compiler_params=pltpu.CompilerParams(
            dimension_semantics=("arbitrary",),
            vmem_limit_bytes=VMEM_LIMIT_BYTES),
        name="sandwich_layer",
    )(x, x, pre, win, cw, cb, wp, ps, wout, post)


def _pair_block_diag(w_pool):
    depth, groups, c, _ = w_pool.shape
    wp = w_pool.reshape(depth, groups // 2, 2, c, c)
    z = jnp.zeros_like(wp[:, :, 0])
    top = jnp.concatenate([wp[:, :, 0], z], axis=-1)
    bot = jnp.concatenate([z, wp[:, :, 1]], axis=-1)
    return jnp.concatenate([top, bot], axis=-2)


def kernel(x, pre_norm, w_in, conv_w, conv_b, w_pool, pool_scale, w_out, post_norm):
    depth = w_in.shape[0]
    win = w_in.astype(jnp.bfloat16)
    wout = w_out.astype(jnp.bfloat16)
    wp = _pair_block_diag(w_pool).astype(jnp.bfloat16)
    pre = pre_norm[:, None, :]
    cb = conv_b[:, None, :]
    ps = pool_scale[:, None, :]
    post = post_norm[:, None, :]
    for l in range(depth):
        x = _layer(l, x, pre, win, conv_w, cb, wp, ps, wout, post)
    return x
```

```python
import functools

import jax
import jax.numpy as jnp
from jax import lax
from jax.experimental import pallas as pl
from jax.experimental.pallas import tpu as pltpu

RMS_EPS = 1e-6
CONV_K = 3
POOL_WINDOWS = (2, 4, 8, 16)
N_POOL_GROUPS = len(POOL_WINDOWS)

SEQ_TILE = 512
OUT_ROWS = 256
LANES = 128
CONV_HALO = 8
POOL_HALO = 16
POOL_PAD = 8
POOL_TMPS = 5
WEIGHT_CAST_ROWS = 128
VMEM_LIMIT_BYTES = 58 * 1024 * 1024


def _rms_inv(v):
    return lax.rsqrt(jnp.mean(v * v, axis=-1, keepdims=True) + RMS_EPS)


def _silu_of_twice(h):
    return h * jnp.tanh(h) + h


def _shifted_rows(ref, start, rows):
    return ref[pl.ds(start, rows, stride=1), :]


ST_U, ST_C, ST_H, ST_ZA, ST_ZB, ST_B = range(6)
N_STAGE = 6


def _mix_stage(j, x_ref, pre_ref, win_ref, cw_ref, cb_ref, wp_ref, ps_ref,
               vext_ref, uext_ref, tmp_ref, stage_ref, yb_ref, p_ref, y_ref, *,
               ts, a, tiles_per_seq, first_step):
    p_ch = N_POOL_GROUPS * LANES
    body = POOL_PAD + POOL_HALO
    ext = POOL_HALO + ts
    if first_step:
        vext_ref[:, 0:CONV_HALO, :] = jnp.zeros((a // LANES, CONV_HALO, LANES), jnp.float32)
        uext_ref[:, 0:body, :] = jnp.zeros((N_POOL_GROUPS, body, LANES), jnp.float32)
        tmp_ref[:, 0:POOL_PAD, :] = jnp.zeros((tmp_ref.shape[0], POOL_PAD, LANES), jnp.float32)
    carry_on = j + 1 < tiles_per_seq

    x = x_ref[0]
    xg = (x * pre_ref[...]).astype(jnp.bfloat16)

    def proj_to(slot, lo):
        stage_ref[slot] = jnp.dot(xg, win_ref[:, lo:lo + a], preferred_element_type=jnp.float32)

    proj_to(ST_U, 4 * a)
    yield
    inv = _rms_inv(x)
    row = lax.broadcasted_iota(jnp.int32, (POOL_HALO, LANES), 0)
    pos1 = (j * ts + row + 1).astype(jnp.float32)
    n_tmp = 0
    for g, w in enumerate(POOL_WINDOWS):
        lo = g * LANES
        ug = stage_ref[ST_U, :, lo:lo + LANES] * inv
        ue = uext_ref.at[g]
        ue[body:body + ts, :] = ug
        s = ue[POOL_PAD:POOL_PAD + ext, :] + _shifted_rows(ue, POOL_PAD - 1, ext)
        m = 2
        while m < w and m < 8:
            t = tmp_ref.at[n_tmp]
            n_tmp += 1
            t[POOL_PAD:POOL_PAD + ext, :] = s
            s = s + _shifted_rows(t, POOL_PAD - m, ext)
            m *= 2
        if m < w:
            s = s + jnp.concatenate([s[:8], s[:-8]], axis=0)
        s = s[POOL_HALO:]
        p_body = s * (1.0 / w) - ug
        p_head = s[0:POOL_HALO] / jnp.minimum(pos1, float(w)) - ug[0:POOL_HALO]
        p = jnp.concatenate([p_head, p_body[POOL_HALO:]], axis=0)
        p_ref[:, lo:lo + LANES] = p.astype(jnp.bfloat16)
        ue[POOL_PAD:body, :] = jnp.where(carry_on, ug[ts - POOL_HALO:], 0.0)

    proj_to(ST_C, a)
    yield
    proj_to(ST_H, 2 * a)
    yield
    inv2 = inv * inv
    half_inv = 0.5 * inv
    conv = []
    for q in range(a // LANES):
        lo = q * LANES
        vq = (stage_ref[ST_C, :, lo:lo + LANES] * stage_ref[ST_H, :, lo:lo + LANES]) * inv2
        ve = vext_ref.at[q]
        ve[CONV_HALO:CONV_HALO + ts, :] = vq
        conv.append((cw_ref[0:1, lo:lo + LANES] * _shifted_rows(ve, CONV_HALO - 2, ts)
                     + cw_ref[1:2, lo:lo + LANES] * _shifted_rows(ve, CONV_HALO - 1, ts)
                     + cw_ref[2:3, lo:lo + LANES] * vq) + cb_ref[:, lo:lo + LANES])
        ve[0:CONV_HALO, :] = jnp.where(carry_on, vq[ts - CONV_HALO:], 0.0)
    conv = jnp.concatenate(conv, axis=1)
    proj_to(ST_ZA, 3 * a)
    yield
    stage_ref[ST_ZA] = conv * _silu_of_twice(stage_ref[ST_ZA] * half_inv)
    proj_to(ST_ZB, 4 * a + p_ch)
    yield
    stage_ref[ST_ZB] = _silu_of_twice(stage_ref[ST_ZB] * half_inv) * ps_ref[...]
    proj_to(ST_B, 0)
    yield
    y_ref[:, 0:a] = ((stage_ref[ST_B] * inv) * stage_ref[ST_ZA]).astype(jnp.bfloat16)

    for hh in range(wp_ref.shape[0]):
        lo = hh * 2 * LANES
        yb_ref[:, lo:lo + 2 * LANES] = jnp.dot(p_ref[:, lo:lo + 2 * LANES], wp_ref[hh],
                                               preferred_element_type=jnp.float32)
    yield
    y_ref[:, a:a + p_ch] = (yb_ref[...] * stage_ref[ST_ZB]).astype(jnp.bfloat16)


def _out_stage(y_ref, xres_ref, wout_ref, post_ref, o_ref, so_ref, *, ts):
    for r in range(0, ts, OUT_ROWS):
        so_ref[r:r + OUT_ROWS, :] = jnp.dot(y_ref[r:r + OUT_ROWS, :], wout_ref[...],
                                            preferred_element_type=jnp.float32)
        yield
        out = so_ref[r:r + OUT_ROWS, :]
        o_ref[0, r:r + OUT_ROWS, :] = xres_ref[0, r:r + OUT_ROWS, :] + (out * _rms_inv(out)) * post_ref[...]


def _interleave(first, second):
    pending = [first, second]
    while pending:
        for gen in list(pending):
            try:
                next(gen)
            except StopIteration:
                pending.remove(gen)


def _cast_weights(win32_ref, wp32_ref, wout32_ref, win_ref, wp_ref, wout_ref):
    for r in range(0, win32_ref.shape[0], WEIGHT_CAST_ROWS):
        win_ref[r:r + WEIGHT_CAST_ROWS, :] = win32_ref[r:r + WEIGHT_CAST_ROWS, :].astype(jnp.bfloat16)
    for r in range(0, wout32_ref.shape[0], WEIGHT_CAST_ROWS):
        wout_ref[r:r + WEIGHT_CAST_ROWS, :] = wout32_ref[r:r + WEIGHT_CAST_ROWS, :].astype(jnp.bfloat16)
    c = wp32_ref.shape[-1]
    wp_ref[...] = jnp.zeros(wp_ref.shape, jnp.bfloat16)
    for g in range(wp32_ref.shape[0]):
        lo = (g % 2) * c
        wp_ref[g // 2, lo:lo + c, lo:lo + c] = wp32_ref[g].astype(jnp.bfloat16)


def _layer_kernel(x_ref, xres_ref, pre_ref, win32_ref, cw_ref, cb_ref, wp32_ref, ps_ref, wout32_ref,
                  post_ref, o_ref, win_ref, wp_ref, wout_ref, vext_ref, uext_ref, tmp_ref, stage_ref,
                  yb_ref, so_ref, p_ref, y_ref, *, seq_tile, conv_ch, tiles_per_seq, n_tiles):
    s = pl.program_id(0)

    @pl.when(s == 0)
    def _():
        _cast_weights(win32_ref, wp32_ref, wout32_ref, win_ref, wp_ref, wout_ref)

    mix = functools.partial(
        _mix_stage, s % tiles_per_seq, x_ref, pre_ref, win_ref, cw_ref, cb_ref, wp_ref, ps_ref,
        vext_ref, uext_ref, tmp_ref, stage_ref, yb_ref, p_ref, y_ref.at[s % 2], ts=seq_tile, a=conv_ch,
        tiles_per_seq=tiles_per_seq)
    out = functools.partial(
        _out_stage, y_ref.at[(s + 1) % 2], xres_ref, wout_ref, post_ref, o_ref, so_ref, ts=seq_tile)

    @pl.when(s == 0)
    def _():
        _interleave(mix(first_step=True), iter(()))

    @pl.when(jnp.logical_and(s > 0, s < n_tiles))
    def _():
        _interleave(mix(first_step=False), out())

    @pl.when(s == n_tiles)
    def _():
        _interleave(out(), iter(()))


def _layer(l, x, pre, win, cw, cb, wp, ps, wout, post):
    bsz, seq, d = x.shape
    a = cw.shape[-1]
    p_ch = ps.shape[-1]
    pool_gc = wp.shape[-1]
    in_cols = win.shape[-1]
    ts = SEQ_TILE
    assert seq % ts == 0 and ts % OUT_ROWS == 0
    assert in_cols == 4 * a + 2 * p_ch and p_ch == N_POOL_GROUPS * pool_gc
    assert pool_gc == LANES and a % LANES == 0 and p_ch == a
    tiles_per_seq = seq // ts
    n_tiles = bsz * tiles_per_seq

    def layer_block(*shape):
        return pl.BlockSpec((None,) + shape, lambda s: (l,) + (0,) * len(shape),
                            pipeline_mode=pl.Buffered(1))

    def tile_index(t):
        return (t // tiles_per_seq, t % tiles_per_seq, 0)

    mix_tile = lambda s: tile_index(jnp.minimum(s, n_tiles - 1))
    out_tile = lambda s: tile_index(jnp.maximum(s - 1, 0))

    kernel = functools.partial(_layer_kernel, seq_tile=ts, conv_ch=a,
                               tiles_per_seq=tiles_per_seq, n_tiles=n_tiles)
    return pl.pallas_call(
        kernel,
        grid=(n_tiles + 1,),
        in_specs=[
            pl.BlockSpec((1, ts, d), mix_tile),
            pl.BlockSpec((1, ts, d), out_tile),
            layer_block(1, d),
            layer_block(d, in_cols),
            layer_block(CONV_K, a),
            layer_block(1, a),
            layer_block(N_POOL_GROUPS, pool_gc, pool_gc),
            layer_block(1, p_ch),
            layer_block(a + p_ch, d),
            layer_block(1, d),
        ],
        out_specs=pl.BlockSpec((1, ts, d), out_tile),
        out_shape=jax.ShapeDtypeStruct(x.shape, x.dtype),
        scratch_shapes=[
            pltpu.VMEM((d, in_cols), jnp.bfloat16),
            pltpu.VMEM((N_POOL_GROUPS // 2, 2 * pool_gc, 2 * pool_gc), jnp.bfloat16),
            pltpu.VMEM((a + p_ch, d), jnp.bfloat16),
            pltpu.VMEM((a // LANES, CONV_HALO + ts, LANES), jnp.float32),
            pltpu.VMEM((N_POOL_GROUPS, POOL_PAD + POOL_HALO + ts, LANES), jnp.float32),
            pltpu.VMEM((POOL_TMPS, POOL_PAD + POOL_HALO + ts, LANES), jnp.float32),
            pltpu.VMEM((N_STAGE, ts, a), jnp.float32),
            pltpu.VMEM((ts, p_ch), jnp.float32),
            pltpu.VMEM((ts, d), jnp.float32),
            pltpu.VMEM((ts, p_ch), jnp.bfloat16),
            pltpu.VMEM((2, ts, a + p_ch), jnp.bfloat16),
        ],
        compiler_params=pltpu.CompilerParams(
            dimension_semantics=("arbitrary",),
            vmem_limit_bytes=VMEM_LIMIT_BYTES),
        name="sandwich_layer",
    )(x, x, pre, win, cw, cb, wp, ps, wout, post)


def kernel(x, pre_norm, w_in, conv_w, conv_b, w_pool, pool_scale, w_out, post_norm):
    depth = w_in.shape[0]
    pre = pre_norm[:, None, :]
    cb = conv_b[:, None, :]
    ps = pool_scale[:, None, :]
    post = post_norm[:, None, :]
    for l in range(depth):
        x = _layer(l, x, pre, w_in, conv_w, cb, w_pool, ps, w_out, post)
    return x
```

```python
import functools

import jax
import jax.numpy as jnp
from jax import lax
from jax.experimental import pallas as pl
from jax.experimental.pallas import tpu as pltpu

RMS_EPS = 1e-6
CONV_K = 3
POOL_WINDOWS = (2, 4, 8, 16)
N_POOL_GROUPS = len(POOL_WINDOWS)

SEQ_TILE = 512
OUT_ROWS = 256
LANES = 128
CONV_HALO = 8
POOL_HALO = 16
POOL_PAD = 8
POOL_TMPS = 5
WEIGHT_CAST_ROWS = 128
VMEM_LIMIT_BYTES = 58 * 1024 * 1024


def _rms_inv(v):
    return lax.rsqrt(jnp.mean(v * v, axis=-1, keepdims=True) + RMS_EPS)


def _silu_of_twice(h):
    return h * jnp.tanh(h) + h


def _shifted_rows(ref, start, rows):
    return ref[pl.ds(start, rows, stride=1), :]


ST_U, ST_C, ST_H, ST_ZA, ST_ZB, ST_B = range(6)
N_STAGE = 6


def _mix_stage(j, x_ref, pre_ref, win_ref, cw_ref, cb_ref, wp_ref, ps_ref,
               vext_ref, uext_ref, tmp_ref, stage_ref, yb_ref, p_ref, y_ref, *,
               ts, a, tiles_per_seq, first_step):
    p_ch = N_POOL_GROUPS * LANES
    body = POOL_PAD + POOL_HALO
    ext = POOL_HALO + ts
    if first_step:
        vext_ref[:, 0:CONV_HALO, :] = jnp.zeros((a // LANES, CONV_HALO, LANES), jnp.float32)
        uext_ref[:, 0:body, :] = jnp.zeros((N_POOL_GROUPS, body, LANES), jnp.float32)
        tmp_ref[:, 0:POOL_PAD, :] = jnp.zeros((tmp_ref.shape[0], POOL_PAD, LANES), jnp.float32)
    carry_on = j + 1 < tiles_per_seq

    x = x_ref[0]
    xg = (x * pre_ref[...]).astype(jnp.bfloat16)

    def proj_to(slot, lo):
        stage_ref[slot] = jnp.dot(xg, win_ref[:, lo:lo + a], preferred_element_type=jnp.float32)

    proj_to(ST_U, 4 * a)
    yield
    inv = _rms_inv(x)
    row = lax.broadcasted_iota(jnp.int32, (POOL_HALO, LANES), 0)
    pos1 = (j * ts + row + 1).astype(jnp.float32)
    n_tmp = 0
    for g, w in enumerate(POOL_WINDOWS):
        lo = g * LANES
        ug = stage_ref[ST_U, :, lo:lo + LANES] * inv
        ue = uext_ref.at[g]
        ue[body:body + ts, :] = ug
        s = ue[POOL_PAD:POOL_PAD + ext, :] + _shifted_rows(ue, POOL_PAD - 1, ext)
        m = 2
        while m < w and m < 8:
            t = tmp_ref.at[n_tmp]
            n_tmp += 1
            t[POOL_PAD:POOL_PAD + ext, :] = s
            s = s + _shifted_rows(t, POOL_PAD - m, ext)
            m *= 2
        if m < w:
            s = s + jnp.concatenate([s[:8], s[:-8]], axis=0)
        s = s[POOL_HALO:]
        p_body = s * (1.0 / w) - ug
        p_head = s[0:POOL_HALO] / jnp.minimum(pos1, float(w)) - ug[0:POOL_HALO]
        p = jnp.concatenate([p_head, p_body[POOL_HALO:]], axis=0)
        p_ref[:, lo:lo + LANES] = p.astype(jnp.bfloat16)
        ue[POOL_PAD:body, :] = jnp.where(carry_on, ug[ts - POOL_HALO:], 0.0)

    proj_to(ST_C, a)
    yield
    proj_to(ST_H, 2 * a)
    yield
    inv2 = inv * inv
    half_inv = 0.5 * inv
    conv = []
    for q in range(a // LANES):
        lo = q * LANES
        vq = (stage_ref[ST_C, :, lo:lo + LANES] * stage_ref[ST_H, :, lo:lo + LANES]) * inv2
        ve = vext_ref.at[q]
        ve[CONV_HALO:CONV_HALO + ts, :] = vq
        conv.append((cw_ref[0:1, lo:lo + LANES] * _shifted_rows(ve, CONV_HALO - 2, ts)
                     + cw_ref[1:2, lo:lo + LANES] * _shifted_rows(ve, CONV_HALO - 1, ts)
                     + cw_ref[2:3, lo:lo + LANES] * vq) + cb_ref[:, lo:lo + LANES])
        ve[0:CONV_HALO, :] = jnp.where(carry_on, vq[ts - CONV_HALO:], 0.0)
    conv = jnp.concatenate(conv, axis=1)
    proj_to(ST_ZA, 3 * a)
    yield
    stage_ref[ST_ZA] = conv * _silu_of_twice(stage_ref[ST_ZA] * half_inv)
    proj_to(ST_ZB, 4 * a + p_ch)
    yield
    stage_ref[ST_ZB] = _silu_of_twice(stage_ref[ST_ZB] * half_inv) * ps_ref[...]
    proj_to(ST_B, 0)
    yield
    y_ref[:, 0:a] = ((stage_ref[ST_B] * inv) * stage_ref[ST_ZA]).astype(jnp.bfloat16)

    for hh in range(wp_ref.shape[0]):
        lo = hh * 2 * LANES
        yb_ref[:, lo:lo + 2 * LANES] = jnp.dot(p_ref[:, lo:lo + 2 * LANES], wp_ref[hh],
                                               preferred_element_type=jnp.float32)
    yield
    y_ref[:, a:a + p_ch] = (yb_ref[...] * stage_ref[ST_ZB]).astype(jnp.bfloat16)


def _out_stage(y_ref, xres_ref, wout_ref, post_ref, o_ref, so_ref, *, ts):
    for r in range(0, ts, OUT_ROWS):
        so_ref[r:r + OUT_ROWS, :] = jnp.dot(y_ref[r:r + OUT_ROWS, :], wout_ref[...],
                                            preferred_element_type=jnp.float32)
        yield
        out = so_ref[r:r + OUT_ROWS, :]
        o_ref[0, r:r + OUT_ROWS, :] = xres_ref[0, r:r + OUT_ROWS, :] + (out * _rms_inv(out)) * post_ref[...]


def _interleave(first, second):
    pending = [first, second]
    while pending:
        for gen in list(pending):
            try:
                next(gen)
            except StopIteration:
                pending.remove(gen)


def _cast_weights(win32_ref, wp32_ref, wout32_ref, win_ref, wp_ref, wout_ref):
    for r in range(0, win32_ref.shape[0], WEIGHT_CAST_ROWS):
        win_ref[r:r + WEIGHT_CAST_ROWS, :] = win32_ref[r:r + WEIGHT_CAST_ROWS, :].astype(jnp.bfloat16)
    for r in range(0, wout32_ref.shape[0], WEIGHT_CAST_ROWS):
        wout_ref[r:r + WEIGHT_CAST_ROWS, :] = wout32_ref[r:r + WEIGHT_CAST_ROWS, :].astype(jnp.bfloat16)
    c = wp32_ref.shape[-1]
    wp_ref[...] = jnp.zeros(wp_ref.shape, jnp.bfloat16)
    for g in range(wp32_ref.shape[0]):
        lo = (g % 2) * c
        wp_ref[g // 2, lo:lo + c, lo:lo + c] = wp32_ref[g].astype(jnp.bfloat16)


def _layer_kernel(x_ref, xres_ref, pre_ref, win32_ref, cw_ref, cb_ref, wp32_ref, ps_ref, wout32_ref,
                  post_ref, o_ref, win_ref, wp_ref, wout_ref, vext_ref, uext_ref, tmp_ref, stage_ref,
                  yb_ref, so_ref, p_ref, y_ref, *, layer, seq_tile, conv_ch, tiles_per_seq, n_tiles):
    s = pl.program_id(0)
    pre_ref, cb_ref, ps_ref, post_ref = (r.at[pl.ds(layer, 1)] for r in (pre_ref, cb_ref, ps_ref, post_ref))

    @pl.when(s == 0)
    def _():
        _cast_weights(win32_ref, wp32_ref, wout32_ref, win_ref, wp_ref, wout_ref)

    mix = functools.partial(
        _mix_stage, s % tiles_per_seq, x_ref, pre_ref, win_ref, cw_ref, cb_ref, wp_ref, ps_ref,
        vext_ref, uext_ref, tmp_ref, stage_ref, yb_ref, p_ref, y_ref.at[s % 2], ts=seq_tile, a=conv_ch,
        tiles_per_seq=tiles_per_seq)
    out = functools.partial(
        _out_stage, y_ref.at[(s + 1) % 2], xres_ref, wout_ref, post_ref, o_ref, so_ref, ts=seq_tile)

    @pl.when(s == 0)
    def _():
        _interleave(mix(first_step=True), iter(()))

    @pl.when(jnp.logical_and(s > 0, s < n_tiles))
    def _():
        _interleave(mix(first_step=False), out())

    @pl.when(s == n_tiles)
    def _():
        _interleave(out(), iter(()))


def _layer(l, x, pre, win, cw, cb, wp, ps, wout, post):
    bsz, seq, d = x.shape
    a = cw.shape[-1]
    p_ch = ps.shape[-1]
    pool_gc = wp.shape[-1]
    in_cols = win.shape[-1]
    ts = SEQ_TILE
    assert seq % ts == 0 and ts % OUT_ROWS == 0
    assert in_cols == 4 * a + 2 * p_ch and p_ch == N_POOL_GROUPS * pool_gc
    assert pool_gc == LANES and a % LANES == 0 and p_ch == a
    tiles_per_seq = seq // ts
    n_tiles = bsz * tiles_per_seq

    def layer_block(*shape):
        return pl.BlockSpec((None,) + shape, lambda s: (l,) + (0,) * len(shape),
                            pipeline_mode=pl.Buffered(1))

    def tile_index(t):
        return (t // tiles_per_seq, t % tiles_per_seq, 0)

    mix_tile = lambda s: tile_index(jnp.minimum(s, n_tiles - 1))
    out_tile = lambda s: tile_index(jnp.maximum(s - 1, 0))

    def stacked_rows(width):
        return pl.BlockSpec((pre.shape[0], width), lambda s: (0, 0), pipeline_mode=pl.Buffered(1))

    kernel = functools.partial(_layer_kernel, layer=l, seq_tile=ts, conv_ch=a,
                               tiles_per_seq=tiles_per_seq, n_tiles=n_tiles)
    return pl.pallas_call(
        kernel,
        grid=(n_tiles + 1,),
        in_specs=[
            pl.BlockSpec((1, ts, d), mix_tile),
            pl.BlockSpec((1, ts, d), out_tile),
            stacked_rows(d),
            layer_block(d, in_cols),
            layer_block(CONV_K, a),
            stacked_rows(a),
            layer_block(N_POOL_GROUPS, pool_gc, pool_gc),
            stacked_rows(p_ch),
            layer_block(a + p_ch, d),
            stacked_rows(d),
        ],
        out_specs=pl.BlockSpec((1, ts, d), out_tile),
        out_shape=jax.ShapeDtypeStruct(x.shape, x.dtype),
        scratch_shapes=[
            pltpu.VMEM((d, in_cols), jnp.bfloat16),
            pltpu.VMEM((N_POOL_GROUPS // 2, 2 * pool_gc, 2 * pool_gc), jnp.bfloat16),
            pltpu.VMEM((a + p_ch, d), jnp.bfloat16),
            pltpu.VMEM((a // LANES, CONV_HALO + ts, LANES), jnp.float32),
            pltpu.VMEM((N_POOL_GROUPS, POOL_PAD + POOL_HALO + ts, LANES), jnp.float32),
            pltpu.VMEM((POOL_TMPS, POOL_PAD + POOL_HALO + ts, LANES), jnp.float32),
            pltpu.VMEM((N_STAGE, ts, a), jnp.float32),
            pltpu.VMEM((ts, p_ch), jnp.float32),
            pltpu.VMEM((ts, d), jnp.float32),
            pltpu.VMEM((ts, p_ch), jnp.bfloat16),
            pltpu.VMEM((2, ts, a + p_ch), jnp.bfloat16),
        ],
        compiler_params=pltpu.CompilerParams(
            dimension_semantics=("arbitrary",),
            vmem_limit_bytes=VMEM_LIMIT_BYTES),
        name="sandwich_layer",
    )(x, x, pre, win, cw, cb, wp, ps, wout, post)


def kernel(x, pre_norm, w_in, conv_w, conv_b, w_pool, pool_scale, w_out, post_norm):
    depth = w_in.shape[0]
    for l in range(depth):
        x = _layer(l, x, pre_norm, w_in, conv_w, conv_b, w_pool, pool_scale, w_out, post_norm)
    return x
```

```python
import functools

import jax
import jax.numpy as jnp
from jax import lax
from jax.experimental import pallas as pl
from jax.experimental.pallas import tpu as pltpu

RMS_EPS = 1e-6
CONV_K = 3
POOL_WINDOWS = (2, 4, 8, 16)
N_POOL_GROUPS = len(POOL_WINDOWS)

SEQ_TILE = 512
OUT_ROWS = 256
LANES = 128
CONV_HALO = 8
POOL_HALO = 16
POOL_PAD = 8
POOL_TMPS = 5
WSTAGE_SLOTS = 2
VMEM_LIMIT_BYTES = 58 * 1024 * 1024


def _rms_inv(v):
    return lax.rsqrt(jnp.mean(v * v, axis=-1, keepdims=True) + RMS_EPS)


def _silu_of_twice(h):
    return h * jnp.tanh(h) + h


def _shifted_rows(ref, start, rows):
    return ref[pl.ds(start, rows, stride=1), :]


ST_U, ST_C, ST_H, ST_ZA, ST_ZB, ST_B = range(6)
N_STAGE = 6


def _mix_stage(j, x_ref, pre_ref, win_ref, cw_ref, cb_ref, wp_ref, ps_ref,
               vext_ref, uext_ref, tmp_ref, stage_ref, yb_ref, p_ref, y_ref, *,
               ts, a, tiles_per_seq, first_step):
    p_ch = N_POOL_GROUPS * LANES
    body = POOL_PAD + POOL_HALO
    ext = POOL_HALO + ts
    if first_step:
        vext_ref[:, 0:CONV_HALO, :] = jnp.zeros((a // LANES, CONV_HALO, LANES), jnp.float32)
        uext_ref[:, 0:body, :] = jnp.zeros((N_POOL_GROUPS, body, LANES), jnp.float32)
        tmp_ref[:, 0:POOL_PAD, :] = jnp.zeros((tmp_ref.shape[0], POOL_PAD, LANES), jnp.float32)
    carry_on = j + 1 < tiles_per_seq

    x = x_ref[0]
    xg = (x * pre_ref[...]).astype(jnp.bfloat16)

    def proj_to(slot, lo):
        stage_ref[slot] = jnp.dot(xg, win_ref[:, lo:lo + a], preferred_element_type=jnp.float32)

    proj_to(ST_U, 4 * a)
    yield
    inv = _rms_inv(x)
    row = lax.broadcasted_iota(jnp.int32, (POOL_HALO, LANES), 0)
    pos1 = (j * ts + row + 1).astype(jnp.float32)
    n_tmp = 0
    for g, w in enumerate(POOL_WINDOWS):
        lo = g * LANES
        ug = stage_ref[ST_U, :, lo:lo + LANES] * inv
        ue = uext_ref.at[g]
        ue[body:body + ts, :] = ug
        s = ue[POOL_PAD:POOL_PAD + ext, :] + _shifted_rows(ue, POOL_PAD - 1, ext)
        m = 2
        while m < w and m < 8:
            t = tmp_ref.at[n_tmp]
            n_tmp += 1
            t[POOL_PAD:POOL_PAD + ext, :] = s
            s = s + _shifted_rows(t, POOL_PAD - m, ext)
            m *= 2
        if m < w:
            s = s + jnp.concatenate([s[:8], s[:-8]], axis=0)
        s = s[POOL_HALO:]
        p_body = s * (1.0 / w) - ug
        p_head = s[0:POOL_HALO] / jnp.minimum(pos1, float(w)) - ug[0:POOL_HALO]
        p = jnp.concatenate([p_head, p_body[POOL_HALO:]], axis=0)
        p_ref[:, lo:lo + LANES] = p.astype(jnp.bfloat16)
        ue[POOL_PAD:body, :] = jnp.where(carry_on, ug[ts - POOL_HALO:], 0.0)

    proj_to(ST_C, a)
    yield
    proj_to(ST_H, 2 * a)
    yield
    inv2 = inv * inv
    half_inv = 0.5 * inv
    conv = []
    for q in range(a // LANES):
        lo = q * LANES
        vq = (stage_ref[ST_C, :, lo:lo + LANES] * stage_ref[ST_H, :, lo:lo + LANES]) * inv2
        ve = vext_ref.at[q]
        ve[CONV_HALO:CONV_HALO + ts, :] = vq
        conv.append((cw_ref[0:1, lo:lo + LANES] * _shifted_rows(ve, CONV_HALO - 2, ts)
                     + cw_ref[1:2, lo:lo + LANES] * _shifted_rows(ve, CONV_HALO - 1, ts)
                     + cw_ref[2:3, lo:lo + LANES] * vq) + cb_ref[:, lo:lo + LANES])
        ve[0:CONV_HALO, :] = jnp.where(carry_on, vq[ts - CONV_HALO:], 0.0)
    conv = jnp.concatenate(conv, axis=1)
    proj_to(ST_ZA, 3 * a)
    yield
    stage_ref[ST_ZA] = conv * _silu_of_twice(stage_ref[ST_ZA] * half_inv)
    proj_to(ST_ZB, 4 * a + p_ch)
    yield
    stage_ref[ST_ZB] = _silu_of_twice(stage_ref[ST_ZB] * half_inv) * ps_ref[...]
    proj_to(ST_B, 0)
    yield
    y_ref[:, 0:a] = ((stage_ref[ST_B] * inv) * stage_ref[ST_ZA]).astype(jnp.bfloat16)

    for hh in range(wp_ref.shape[0]):
        lo = hh * 2 * LANES
        yb_ref[:, lo:lo + 2 * LANES] = jnp.dot(p_ref[:, lo:lo + 2 * LANES], wp_ref[hh],
                                               preferred_element_type=jnp.float32)
    yield
    y_ref[:, a:a + p_ch] = (yb_ref[...] * stage_ref[ST_ZB]).astype(jnp.bfloat16)


def _out_stage(y_ref, xres_ref, wout_ref, post_ref, o_ref, so_ref, *, ts):
    for r in range(0, ts, OUT_ROWS):
        so_ref[r:r + OUT_ROWS, :] = jnp.dot(y_ref[r:r + OUT_ROWS, :], wout_ref[...],
                                            preferred_element_type=jnp.float32)
        yield
        out = so_ref[r:r + OUT_ROWS, :]
        o_ref[0, r:r + OUT_ROWS, :] = xres_ref[0, r:r + OUT_ROWS, :] + (out * _rms_inv(out)) * post_ref[...]


def _interleave(first, second):
    pending = [first, second]
    while pending:
        for gen in list(pending):
            try:
                next(gen)
            except StopIteration:
                pending.remove(gen)


def _weight_feeder(layer, win_hbm, wout_hbm, wp_hbm, cw_hbm, wstage_ref, wp32_ref, cw_ref, sem, win_ref, wp_ref,
                   *, a):
    in_cols = _in_segment_cols(a)
    copies = [_segment_copy(win_hbm, layer, lo, wstage_ref, sem, k) for k, lo in enumerate(in_cols)]
    copies += _out_weight_copies(wout_hbm, layer, wstage_ref, sem, first=len(in_cols))
    wp_copy = pltpu.make_async_copy(wp_hbm.at[layer], wp32_ref, sem.at[WSTAGE_SLOTS])
    cw_copy = pltpu.make_async_copy(cw_hbm.at[layer], cw_ref, sem.at[WSTAGE_SLOTS + 1])
    copies[0].start()
    copies[1].start()
    wp_copy.start()
    cw_copy.start()
    for k, lo in enumerate(in_cols):
        if k == 2:
            cw_copy.wait()
        copies[k].wait()
        win_ref[:, lo:lo + a] = wstage_ref[k % WSTAGE_SLOTS].astype(jnp.bfloat16)
        copies[k + WSTAGE_SLOTS].start()
        yield
    wp_copy.wait()
    c = wp32_ref.shape[-1]
    wp_ref[...] = jnp.zeros(wp_ref.shape, jnp.bfloat16)
    for g in range(wp32_ref.shape[0]):
        lo = (g % 2) * c
        wp_ref[g // 2, lo:lo + c, lo:lo + c] = wp32_ref[g].astype(jnp.bfloat16)


def _in_segment_cols(a):
    p_ch = N_POOL_GROUPS * LANES
    return (4 * a, a, 2 * a, 3 * a, 4 * a + p_ch, 0)


def _segment_copy(w_hbm, layer, lo, wstage_ref, sem, k):
    slot = k % WSTAGE_SLOTS
    return pltpu.make_async_copy(w_hbm.at[layer, :, pl.ds(lo, wstage_ref.shape[-1])],
                                 wstage_ref.at[slot], sem.at[slot])


def _out_weight_copies(wout_hbm, layer, wstage_ref, sem, *, first):
    width = wstage_ref.shape[-1]
    return [_segment_copy(wout_hbm, layer, h * width, wstage_ref, sem, first + h)
            for h in range(wout_hbm.shape[-1] // width)]


def _finish_out_weights(layer, wout_hbm, wstage_ref, sem, wout_ref, *, first):
    width = wstage_ref.shape[-1]
    for h, cp in enumerate(_out_weight_copies(wout_hbm, layer, wstage_ref, sem, first=first)):
        cp.wait()
        wout_ref[:, h * width:(h + 1) * width] = wstage_ref[(first + h) % WSTAGE_SLOTS].astype(jnp.bfloat16)


def _layer_kernel(x_ref, xres_ref, pre_ref, win_hbm, cw_hbm, cb_ref, wp_hbm, ps_ref, wout_hbm,
                  post_ref, o_ref, win_ref, wp_ref, wout_ref, wstage_ref, wp32_ref, cw_ref, sem, vext_ref, uext_ref,
                  tmp_ref, stage_ref, yb_ref, so_ref, p_ref, y_ref, *,
                  layer, seq_tile, conv_ch, tiles_per_seq, n_tiles):
    s = pl.program_id(0)
    pre_ref, cb_ref, ps_ref, post_ref = (r.at[pl.ds(layer, 1)] for r in (pre_ref, cb_ref, ps_ref, post_ref))

    mix = functools.partial(
        _mix_stage, s % tiles_per_seq, x_ref, pre_ref, win_ref, cw_ref, cb_ref, wp_ref, ps_ref,
        vext_ref, uext_ref, tmp_ref, stage_ref, yb_ref, p_ref, y_ref.at[s % 2], ts=seq_tile, a=conv_ch,
        tiles_per_seq=tiles_per_seq)
    out = functools.partial(
        _out_stage, y_ref.at[(s + 1) % 2], xres_ref, wout_ref, post_ref, o_ref, so_ref, ts=seq_tile)

    @pl.when(s == 0)
    def _():
        feeder = _weight_feeder(layer, win_hbm, wout_hbm, wp_hbm, cw_hbm, wstage_ref, wp32_ref, cw_ref, sem,
                                win_ref, wp_ref, a=conv_ch)
        _interleave(feeder, mix(first_step=True))

    @pl.when(s == 1)
    def _():
        _finish_out_weights(layer, wout_hbm, wstage_ref, sem, wout_ref, first=len(_in_segment_cols(conv_ch)))

    @pl.when(jnp.logical_and(s > 0, s < n_tiles))
    def _():
        _interleave(mix(first_step=False), out())

    @pl.when(s == n_tiles)
    def _():
        _interleave(out(), iter(()))


def _layer(l, x, pre, win, cw, cb, wp, ps, wout, post):
    bsz, seq, d = x.shape
    a = cw.shape[-1]
    p_ch = ps.shape[-1]
    pool_gc = wp.shape[-1]
    in_cols = win.shape[-1]
    ts = SEQ_TILE
    assert seq % ts == 0 and ts % OUT_ROWS == 0
    assert in_cols == 4 * a + 2 * p_ch and p_ch == N_POOL_GROUPS * pool_gc
    assert pool_gc == LANES and a % LANES == 0 and p_ch == a and d == a + p_ch and d % a == 0
    tiles_per_seq = seq // ts
    n_tiles = bsz * tiles_per_seq

    def tile_index(t):
        return (t // tiles_per_seq, t % tiles_per_seq, 0)

    mix_tile = lambda s: tile_index(jnp.minimum(s, n_tiles - 1))
    out_tile = lambda s: tile_index(jnp.maximum(s - 1, 0))

    def stacked_rows(width):
        return pl.BlockSpec((pre.shape[0], width), lambda s: (0, 0), pipeline_mode=pl.Buffered(1))

    kernel = functools.partial(_layer_kernel, layer=l, seq_tile=ts, conv_ch=a,
                               tiles_per_seq=tiles_per_seq, n_tiles=n_tiles)
    return pl.pallas_call(
        kernel,
        grid=(n_tiles + 1,),
        in_specs=[
            pl.BlockSpec((1, ts, d), mix_tile),
            pl.BlockSpec((1, ts, d), out_tile),
            stacked_rows(d),
            pl.BlockSpec(memory_space=pl.ANY),
            pl.BlockSpec(memory_space=pl.ANY),
            stacked_rows(a),
            pl.BlockSpec(memory_space=pl.ANY),
            stacked_rows(p_ch),
            pl.BlockSpec(memory_space=pl.ANY),
            stacked_rows(d),
        ],
        out_specs=pl.BlockSpec((1, ts, d), out_tile),
        out_shape=jax.ShapeDtypeStruct(x.shape, x.dtype),
        scratch_shapes=[
            pltpu.VMEM((d, in_cols), jnp.bfloat16),
            pltpu.VMEM((N_POOL_GROUPS // 2, 2 * pool_gc, 2 * pool_gc), jnp.bfloat16),
            pltpu.VMEM((a + p_ch, d), jnp.bfloat16),
            pltpu.VMEM((WSTAGE_SLOTS, d, a), jnp.float32),
            pltpu.VMEM((N_POOL_GROUPS, pool_gc, pool_gc), jnp.float32),
            pltpu.VMEM((CONV_K, a), jnp.float32),
            pltpu.SemaphoreType.DMA((WSTAGE_SLOTS + 2,)),
            pltpu.VMEM((a // LANES, CONV_HALO + ts, LANES), jnp.float32),
            pltpu.VMEM((N_POOL_GROUPS, POOL_PAD + POOL_HALO + ts, LANES), jnp.float32),
            pltpu.VMEM((POOL_TMPS, POOL_PAD + POOL_HALO + ts, LANES), jnp.float32),
            pltpu.VMEM((N_STAGE, ts, a), jnp.float32),
            pltpu.VMEM((ts, p_ch), jnp.float32),
            pltpu.VMEM((ts, d), jnp.float32),
            pltpu.VMEM((ts, p_ch), jnp.bfloat16),
            pltpu.VMEM((2, ts, a + p_ch), jnp.bfloat16),
        ],
        compiler_params=pltpu.CompilerParams(
            dimension_semantics=("arbitrary",),
            vmem_limit_bytes=VMEM_LIMIT_BYTES),
        name="sandwich_layer",
    )(x, x, pre, win, cw, cb, wp, ps, wout, post)


def kernel(x, pre_norm, w_in, conv_w, conv_b, w_pool, pool_scale, w_out, post_norm):
    depth = w_in.shape[0]
    for l in range(depth):
        x = _layer(l, x, pre_norm, w_in, conv_w, conv_b, w_pool, pool_scale, w_out, post_norm)
    return x
```

```python
import functools

import jax
import jax.numpy as jnp
from jax import lax
from jax.experimental import pallas as pl
from jax.experimental.pallas import tpu as pltpu

RMS_EPS = 1e-6
CONV_K = 3
POOL_WINDOWS = (2, 4, 8, 16)
N_POOL_GROUPS = len(POOL_WINDOWS)

SEQ_TILE = 512
OUT_ROWS = 256
LANES = 128
CONV_HALO = 8
POOL_HALO = 16
POOL_PAD = 8
POOL_TMPS = 5
WSTAGE_SLOTS = 2
PITCH_PAD = LANES
VMEM_LIMIT_BYTES = 58 * 1024 * 1024


def _rms_inv(v):
    return lax.rsqrt(jnp.mean(v * v, axis=-1, keepdims=True) + RMS_EPS)


def _silu_of_twice(h):
    return h * jnp.tanh(h) + h


def _shifted_rows(ref, start, rows):
    return ref[pl.ds(start, rows, stride=1), :]


ST_U, ST_C, ST_H, ST_ZA, ST_ZB, ST_B = range(6)
N_STAGE = 6


def _mix_stage(j, x_ref, pre_ref, win_ref, cw_ref, cb_ref, wp_ref, ps_ref,
               vext_ref, uext_ref, tmp_ref, stage_ref, yb_ref, p_ref, y_ref, *,
               ts, a, tiles_per_seq, first_step):
    p_ch = N_POOL_GROUPS * LANES
    body = POOL_PAD + POOL_HALO
    ext = POOL_HALO + ts
    if first_step:
        vext_ref[:, 0:CONV_HALO, :] = jnp.zeros((a // LANES, CONV_HALO, LANES), jnp.float32)
        uext_ref[:, 0:body, :] = jnp.zeros((N_POOL_GROUPS, body, LANES), jnp.float32)
        tmp_ref[:, 0:POOL_PAD, :] = jnp.zeros((tmp_ref.shape[0], POOL_PAD, LANES), jnp.float32)
    carry_on = j + 1 < tiles_per_seq

    x = x_ref[0]
    xg = (x * pre_ref[...]).astype(jnp.bfloat16)

    def proj_to(slot, lo):
        stage_ref[slot] = jnp.dot(xg, win_ref[:, lo:lo + a], preferred_element_type=jnp.float32)

    proj_to(ST_U, 4 * a)
    yield
    inv = _rms_inv(x)
    row = lax.broadcasted_iota(jnp.int32, (POOL_HALO, LANES), 0)
    pos1 = (j * ts + row + 1).astype(jnp.float32)
    n_tmp = 0
    for g, w in enumerate(POOL_WINDOWS):
        lo = g * LANES
        ug = stage_ref[ST_U, :, lo:lo + LANES] * inv
        ue = uext_ref.at[g]
        ue[body:body + ts, :] = ug
        s = ue[POOL_PAD:POOL_PAD + ext, :] + _shifted_rows(ue, POOL_PAD - 1, ext)
        m = 2
        while m < w and m < 8:
            t = tmp_ref.at[n_tmp]
            n_tmp += 1
            t[POOL_PAD:POOL_PAD + ext, :] = s
            s = s + _shifted_rows(t, POOL_PAD - m, ext)
            m *= 2
        if m < w:
            s = s + jnp.concatenate([s[:8], s[:-8]], axis=0)
        s = s[POOL_HALO:]
        p_body = s * (1.0 / w) - ug
        p_head = s[0:POOL_HALO] / jnp.minimum(pos1, float(w)) - ug[0:POOL_HALO]
        p = jnp.concatenate([p_head, p_body[POOL_HALO:]], axis=0)
        p_ref[:, lo:lo + LANES] = p.astype(jnp.bfloat16)
        ue[POOL_PAD:body, :] = jnp.where(carry_on, ug[ts - POOL_HALO:], 0.0)

    proj_to(ST_C, a)
    yield
    proj_to(ST_H, 2 * a)
    yield
    inv2 = inv * inv
    half_inv = 0.5 * inv
    conv = []
    for q in range(a // LANES):
        lo = q * LANES
        vq = (stage_ref[ST_C, :, lo:lo + LANES] * stage_ref[ST_H, :, lo:lo + LANES]) * inv2
        ve = vext_ref.at[q]
        ve[CONV_HALO:CONV_HALO + ts, :] = vq
        conv.append((cw_ref[0:1, lo:lo + LANES] * _shifted_rows(ve, CONV_HALO - 2, ts)
                     + cw_ref[1:2, lo:lo + LANES] * _shifted_rows(ve, CONV_HALO - 1, ts)
                     + cw_ref[2:3, lo:lo + LANES] * vq) + cb_ref[:, lo:lo + LANES])
        ve[0:CONV_HALO, :] = jnp.where(carry_on, vq[ts - CONV_HALO:], 0.0)
    conv = jnp.concatenate(conv, axis=1)
    proj_to(ST_ZA, 3 * a)
    yield
    stage_ref[ST_ZA] = conv * _silu_of_twice(stage_ref[ST_ZA] * half_inv)
    proj_to(ST_ZB, 4 * a + p_ch)
    yield
    stage_ref[ST_ZB] = _silu_of_twice(stage_ref[ST_ZB] * half_inv) * ps_ref[...]
    proj_to(ST_B, 0)
    yield
    y_ref[:, 0:a] = ((stage_ref[ST_B] * inv) * stage_ref[ST_ZA]).astype(jnp.bfloat16)

    for hh in range(wp_ref.shape[0]):
        lo = hh * 2 * LANES
        yb_ref[:, lo:lo + 2 * LANES] = jnp.dot(p_ref[:, lo:lo + 2 * LANES], wp_ref[hh],
                                               preferred_element_type=jnp.float32)
    yield
    y_ref[:, a:a + p_ch] = (yb_ref[...] * stage_ref[ST_ZB]).astype(jnp.bfloat16)


def _out_stage(y_ref, xres_ref, wout_ref, post_ref, o_ref, so_ref, *, ts):
    for r in range(0, ts, OUT_ROWS):
        so_ref[r:r + OUT_ROWS, :] = jnp.dot(y_ref[r:r + OUT_ROWS, :], wout_ref[...],
                                            preferred_element_type=jnp.float32)
        yield
        out = so_ref[r:r + OUT_ROWS, :]
        o_ref[0, r:r + OUT_ROWS, :] = xres_ref[0, r:r + OUT_ROWS, :] + (out * _rms_inv(out)) * post_ref[...]


def _interleave(first, second):
    pending = [first, second]
    while pending:
        for gen in list(pending):
            try:
                next(gen)
            except StopIteration:
                pending.remove(gen)


def _weight_feeder(layer, win_hbm, wout_hbm, wp_hbm, wstage_ref, wp32_ref, sem, win_ref, wp_ref, *, a):
    in_cols = _in_segment_cols(a)
    copies = [_segment_copy(win_hbm, layer, lo, wstage_ref, sem, k) for k, lo in enumerate(in_cols)]
    copies += _out_weight_copies(wout_hbm, layer, wstage_ref, sem, first=len(in_cols))
    wp_copy = pltpu.make_async_copy(wp_hbm.at[layer], wp32_ref, sem.at[WSTAGE_SLOTS])
    copies[0].start()
    copies[1].start()
    wp_copy.start()
    for k, lo in enumerate(in_cols):
        copies[k].wait()
        win_ref[:, lo:lo + a] = wstage_ref[k % WSTAGE_SLOTS].astype(jnp.bfloat16)
        copies[k + WSTAGE_SLOTS].start()
        yield
    wp_copy.wait()
    c = wp32_ref.shape[-1]
    wp_ref[...] = jnp.zeros(wp_ref.shape, jnp.bfloat16)
    for g in range(wp32_ref.shape[0]):
        lo = (g % 2) * c
        wp_ref[g // 2, lo:lo + c, lo:lo + c] = wp32_ref[g].astype(jnp.bfloat16)


def _in_segment_cols(a):
    p_ch = N_POOL_GROUPS * LANES
    return (4 * a, a, 2 * a, 3 * a, 4 * a + p_ch, 0)


def _segment_copy(w_hbm, layer, lo, wstage_ref, sem, k):
    slot = k % WSTAGE_SLOTS
    return pltpu.make_async_copy(w_hbm.at[layer, :, pl.ds(lo, wstage_ref.shape[-1])],
                                 wstage_ref.at[slot], sem.at[slot])


def _out_weight_copies(wout_hbm, layer, wstage_ref, sem, *, first):
    width = wstage_ref.shape[-1]
    return [_segment_copy(wout_hbm, layer, h * width, wstage_ref, sem, first + h)
            for h in range(wout_hbm.shape[-1] // width)]


def _finish_out_weights(layer, wout_hbm, wstage_ref, sem, wout_ref, *, first):
    width = wstage_ref.shape[-1]
    for h, cp in enumerate(_out_weight_copies(wout_hbm, layer, wstage_ref, sem, first=first)):
        cp.wait()
        wout_ref[:, h * width:(h + 1) * width] = wstage_ref[(first + h) % WSTAGE_SLOTS].astype(jnp.bfloat16)


def _used_lanes(ref):
    return ref.at[(slice(None),) * (len(ref.shape) - 1) + (pl.ds(0, ref.shape[-1] - PITCH_PAD),)]


def _layer_kernel(x_ref, xres_ref, pre_ref, win_hbm, cw_ref, cb_ref, wp_hbm, ps_ref, wout_hbm,
                  post_ref, o_ref, win_ref, wp_ref, wout_ref, wstage_ref, wp32_ref, sem, vext_ref, uext_ref,
                  tmp_ref, stage_ref, yb_ref, so_ref, p_ref, y_ref, *,
                  layer, seq_tile, conv_ch, tiles_per_seq, n_tiles):
    s = pl.program_id(0)
    pre_ref, cb_ref, ps_ref, post_ref = (r.at[pl.ds(layer, 1)] for r in (pre_ref, cb_ref, ps_ref, post_ref))
    win_ref, wout_ref, stage_ref, yb_ref, so_ref, p_ref, y_ref = (
        _used_lanes(r) for r in (win_ref, wout_ref, stage_ref, yb_ref, so_ref, p_ref, y_ref))

    mix = functools.partial(
        _mix_stage, s % tiles_per_seq, x_ref, pre_ref, win_ref, cw_ref, cb_ref, wp_ref, ps_ref,
        vext_ref, uext_ref, tmp_ref, stage_ref, yb_ref, p_ref, y_ref.at[s % 2], ts=seq_tile, a=conv_ch,
        tiles_per_seq=tiles_per_seq)
    out = functools.partial(
        _out_stage, y_ref.at[(s + 1) % 2], xres_ref, wout_ref, post_ref, o_ref, so_ref, ts=seq_tile)

    @pl.when(s == 0)
    def _():
        feeder = _weight_feeder(layer, win_hbm, wout_hbm, wp_hbm, wstage_ref, wp32_ref, sem,
                                win_ref, wp_ref, a=conv_ch)
        _interleave(feeder, mix(first_step=True))

    @pl.when(s == 1)
    def _():
        _finish_out_weights(layer, wout_hbm, wstage_ref, sem, wout_ref, first=len(_in_segment_cols(conv_ch)))

    @pl.when(jnp.logical_and(s > 0, s < n_tiles))
    def _():
        _interleave(mix(first_step=False), out())

    @pl.when(s == n_tiles)
    def _():
        _interleave(out(), iter(()))


def _layer(l, x, pre, win, cw, cb, wp, ps, wout, post):
    bsz, seq, d = x.shape
    a = cw.shape[-1]
    p_ch = ps.shape[-1]
    pool_gc = wp.shape[-1]
    in_cols = win.shape[-1]
    ts = SEQ_TILE
    assert seq % ts == 0 and ts % OUT_ROWS == 0
    assert in_cols == 4 * a + 2 * p_ch and p_ch == N_POOL_GROUPS * pool_gc
    assert pool_gc == LANES and a % LANES == 0 and p_ch == a and d == a + p_ch and d % a == 0
    tiles_per_seq = seq // ts
    n_tiles = bsz * tiles_per_seq

    def layer_block(*shape):
        return pl.BlockSpec((None,) + shape, lambda s: (l,) + (0,) * len(shape),
                            pipeline_mode=pl.Buffered(1))

    def tile_index(t):
        return (t // tiles_per_seq, t % tiles_per_seq, 0)

    mix_tile = lambda s: tile_index(jnp.minimum(s, n_tiles - 1))
    out_tile = lambda s: tile_index(jnp.maximum(s - 1, 0))

    def stacked_rows(width):
        return pl.BlockSpec((pre.shape[0], width), lambda s: (0, 0), pipeline_mode=pl.Buffered(1))

    kernel = functools.partial(_layer_kernel, layer=l, seq_tile=ts, conv_ch=a,
                               tiles_per_seq=tiles_per_seq, n_tiles=n_tiles)
    return pl.pallas_call(
        kernel,
        grid=(n_tiles + 1,),
        in_specs=[
            pl.BlockSpec((1, ts, d), mix_tile),
            pl.BlockSpec((1, ts, d), out_tile),
            stacked_rows(d),
            pl.BlockSpec(memory_space=pl.ANY),
            layer_block(CONV_K, a),
            stacked_rows(a),
            pl.BlockSpec(memory_space=pl.ANY),
            stacked_rows(p_ch),
            pl.BlockSpec(memory_space=pl.ANY),
            stacked_rows(d),
        ],
        out_specs=pl.BlockSpec((1, ts, d), out_tile),
        out_shape=jax.ShapeDtypeStruct(x.shape, x.dtype),
        scratch_shapes=[
            pltpu.VMEM((d, in_cols + PITCH_PAD), jnp.bfloat16),
            pltpu.VMEM((N_POOL_GROUPS // 2, 2 * pool_gc, 2 * pool_gc), jnp.bfloat16),
            pltpu.VMEM((a + p_ch, d + PITCH_PAD), jnp.bfloat16),
            pltpu.VMEM((WSTAGE_SLOTS, d, a), jnp.float32),
            pltpu.VMEM((N_POOL_GROUPS, pool_gc, pool_gc), jnp.float32),
            pltpu.SemaphoreType.DMA((WSTAGE_SLOTS + 1,)),
            pltpu.VMEM((a // LANES, CONV_HALO + ts, LANES), jnp.float32),
            pltpu.VMEM((N_POOL_GROUPS, POOL_PAD + POOL_HALO + ts, LANES), jnp.float32),
            pltpu.VMEM((POOL_TMPS, POOL_PAD + POOL_HALO + ts, LANES), jnp.float32),
            pltpu.VMEM((N_STAGE, ts, a + PITCH_PAD), jnp.float32),
            pltpu.VMEM((ts, p_ch + PITCH_PAD), jnp.float32),
            pltpu.VMEM((ts, d + PITCH_PAD), jnp.float32),
            pltpu.VMEM((ts, p_ch + PITCH_PAD), jnp.bfloat16),
            pltpu.VMEM((2, ts, a + p_ch + PITCH_PAD), jnp.bfloat16),
        ],
        compiler_params=pltpu.CompilerParams(
            dimension_semantics=("arbitrary",),
            vmem_limit_bytes=VMEM_LIMIT_BYTES),
        name="sandwich_layer",
    )(x, x, pre, win, cw, cb, wp, ps, wout, post)


def kernel(x, pre_norm, w_in, conv_w, conv_b, w_pool, pool_scale, w_out, post_norm):
    depth = w_in.shape[0]
    for l in range(depth):
        x = _layer(l, x, pre_norm, w_in, conv_w, conv_b, w_pool, pool_scale, w_out, post_norm)
    return x
```

```python
import functools

import jax
import jax.numpy as jnp
from jax import lax
from jax.experimental import pallas as pl
from jax.experimental.pallas import tpu as pltpu

RMS_EPS = 1e-6
CONV_K = 3
POOL_WINDOWS = (2, 4, 8, 16)
N_POOL_GROUPS = len(POOL_WINDOWS)

SEQ_TILE = 512
OUT_ROWS = 256
LANES = 128
CONV_HALO = 8
POOL_HALO = 16
POOL_PAD = 8
POOL_TMPS = 5
WSTAGE_SLOTS = 2
PITCH_PAD = LANES
VMEM_LIMIT_BYTES = 58 * 1024 * 1024


def _rms_inv(v):
    return lax.rsqrt(jnp.mean(v * v, axis=-1, keepdims=True) + RMS_EPS)


def _silu_of_twice(h):
    return h * jnp.tanh(h) + h


def _shifted_rows(ref, start, rows):
    return ref[pl.ds(start, rows, stride=1), :]


ST_U, ST_C, ST_H, ST_ZA, ST_ZB, ST_B = range(6)
N_STAGE = 6


def _mix_stage(j, x_ref, pre_ref, win_ref, cw_ref, cb_ref, wp_ref, ps_ref,
               vext_ref, uext_ref, tmp_ref, stage_ref, yb_ref, p_ref, y_ref, *,
               ts, a, tiles_per_seq, first_step):
    p_ch = N_POOL_GROUPS * LANES
    body = POOL_PAD + POOL_HALO
    ext = POOL_HALO + ts
    if first_step:
        vext_ref[:, 0:CONV_HALO, :] = jnp.zeros((a // LANES, CONV_HALO, LANES), jnp.float32)
        uext_ref[:, 0:body, :] = jnp.zeros((N_POOL_GROUPS, body, LANES), jnp.float32)
        tmp_ref[:, 0:POOL_PAD, :] = jnp.zeros((tmp_ref.shape[0], POOL_PAD, LANES), jnp.float32)
    carry_on = j + 1 < tiles_per_seq

    x = x_ref[0]
    xg = (x * pre_ref[...]).astype(jnp.bfloat16)

    def proj_to(slot, lo):
        stage_ref[slot] = jnp.dot(xg, win_ref[:, lo:lo + a], preferred_element_type=jnp.float32)

    proj_to(ST_U, 4 * a)
    yield
    inv = _rms_inv(x)
    row = lax.broadcasted_iota(jnp.int32, (POOL_HALO, LANES), 0)
    pos1 = (j * ts + row + 1).astype(jnp.float32)
    n_tmp = 0
    for g, w in enumerate(POOL_WINDOWS):
        lo = g * LANES
        ug = stage_ref[ST_U, :, lo:lo + LANES] * inv
        ue = uext_ref.at[g]
        ue[body:body + ts, :] = ug
        s = ue[POOL_PAD:POOL_PAD + ext, :] + _shifted_rows(ue, POOL_PAD - 1, ext)
        m = 2
        while m < w and m < 8:
            t = tmp_ref.at[n_tmp]
            n_tmp += 1
            t[POOL_PAD:POOL_PAD + ext, :] = s
            s = s + _shifted_rows(t, POOL_PAD - m, ext)
            m *= 2
        if m < w:
            s = s + jnp.concatenate([s[:8], s[:-8]], axis=0)
        s = s[POOL_HALO:]
        p_body = s * (1.0 / w) - ug
        p_head = s[0:POOL_HALO] / jnp.minimum(pos1, float(w)) - ug[0:POOL_HALO]
        p = jnp.concatenate([p_head, p_body[POOL_HALO:]], axis=0)
        p_ref[:, lo:lo + LANES] = p.astype(jnp.bfloat16)
        ue[POOL_PAD:body, :] = jnp.where(carry_on, ug[ts - POOL_HALO:], 0.0)

    proj_to(ST_C, a)
    yield
    proj_to(ST_H, 2 * a)
    yield
    inv2 = inv * inv
    half_inv = 0.5 * inv
    conv = []
    for q in range(a // LANES):
        lo = q * LANES
        vq = (stage_ref[ST_C, :, lo:lo + LANES] * stage_ref[ST_H, :, lo:lo + LANES]) * inv2
        ve = vext_ref.at[q]
        ve[CONV_HALO:CONV_HALO + ts, :] = vq
        conv.append((cw_ref[0:1, lo:lo + LANES] * _shifted_rows(ve, CONV_HALO - 2, ts)
                     + cw_ref[1:2, lo:lo + LANES] * _shifted_rows(ve, CONV_HALO - 1, ts)
                     + cw_ref[2:3, lo:lo + LANES] * vq) + cb_ref[:, lo:lo + LANES])
        ve[0:CONV_HALO, :] = jnp.where(carry_on, vq[ts - CONV_HALO:], 0.0)
    conv = jnp.concatenate(conv, axis=1)
    proj_to(ST_ZA, 3 * a)
    yield
    stage_ref[ST_ZA] = conv * _silu_of_twice(stage_ref[ST_ZA] * half_inv)
    proj_to(ST_ZB, 4 * a + p_ch)
    yield
    stage_ref[ST_ZB] = _silu_of_twice(stage_ref[ST_ZB] * half_inv) * ps_ref[...]
    proj_to(ST_B, 0)
    yield
    y_ref[:, 0:a] = ((stage_ref[ST_B] * inv) * stage_ref[ST_ZA]).astype(jnp.bfloat16)

    for hh in range(wp_ref.shape[0]):
        lo = hh * 2 * LANES
        yb_ref[:, lo:lo + 2 * LANES] = jnp.dot(p_ref[:, lo:lo + 2 * LANES], wp_ref[hh],
                                               preferred_element_type=jnp.float32)
    yield
    y_ref[:, a:a + p_ch] = (yb_ref[...] * stage_ref[ST_ZB]).astype(jnp.bfloat16)


def _out_stage(y_ref, xres_ref, wout_ref, post_ref, o_ref, so_ref, *, ts):
    for r in range(0, ts, OUT_ROWS):
        so_ref[r:r + OUT_ROWS, :] = jnp.dot(y_ref[r:r + OUT_ROWS, :], wout_ref[...],
                                            preferred_element_type=jnp.float32)
        yield
        out = so_ref[r:r + OUT_ROWS, :]
        o_ref[0, r:r + OUT_ROWS, :] = xres_ref[0, r:r + OUT_ROWS, :] + (out * _rms_inv(out)) * post_ref[...]


def _interleave(first, second):
    pending = [first, second]
    while pending:
        for gen in list(pending):
            try:
                next(gen)
            except StopIteration:
                pending.remove(gen)


def _weight_feeder(layer, win_hbm, wout_hbm, wp_hbm, wstage_ref, wp32_ref, sem, win_ref, wp_ref, *, a):
    in_cols = _in_segment_cols(a)
    copies = [_segment_copy(win_hbm, layer, lo, wstage_ref, sem, k) for k, lo in enumerate(in_cols)]
    copies += _out_weight_copies(wout_hbm, layer, wstage_ref, sem, first=len(in_cols))
    wp_copy = pltpu.make_async_copy(wp_hbm.at[layer], wp32_ref, sem.at[WSTAGE_SLOTS])
    copies[0].start()
    copies[1].start()
    wp_copy.start()
    for k, lo in enumerate(in_cols):
        copies[k].wait()
        win_ref[:, lo:lo + a] = wstage_ref[k % WSTAGE_SLOTS].astype(jnp.bfloat16)
        copies[k + WSTAGE_SLOTS].start()
        yield
    wp_copy.wait()
    c = wp32_ref.shape[-1]
    wp_ref[...] = jnp.zeros(wp_ref.shape, jnp.bfloat16)
    for g in range(wp32_ref.shape[0]):
        lo = (g % 2) * c
        wp_ref[g // 2, lo:lo + c, lo:lo + c] = wp32_ref[g].astype(jnp.bfloat16)


def _in_segment_cols(a):
    p_ch = N_POOL_GROUPS * LANES
    return (4 * a, a, 2 * a, 3 * a, 4 * a + p_ch, 0)


def _segment_copy(w_hbm, layer, lo, wstage_ref, sem, k):
    slot = k % WSTAGE_SLOTS
    return pltpu.make_async_copy(w_hbm.at[layer, :, pl.ds(lo, wstage_ref.shape[-1])],
                                 wstage_ref.at[slot], sem.at[slot])


def _out_weight_copies(wout_hbm, layer, wstage_ref, sem, *, first):
    width = wstage_ref.shape[-1]
    return [_segment_copy(wout_hbm, layer, h * width, wstage_ref, sem, first + h)
            for h in range(wout_hbm.shape[-1] // width)]


def _finish_out_weights(layer, wout_hbm, wstage_ref, sem, wout_ref, *, first):
    width = wstage_ref.shape[-1]
    for h, cp in enumerate(_out_weight_copies(wout_hbm, layer, wstage_ref, sem, first=first)):
        cp.wait()
        wout_ref[:, h * width:(h + 1) * width] = wstage_ref[(first + h) % WSTAGE_SLOTS].astype(jnp.bfloat16)


def _used_lanes(ref):
    return ref.at[(slice(None),) * (len(ref.shape) - 1) + (pl.ds(0, ref.shape[-1] - PITCH_PAD),)]


def _layer_kernel(x_ref, xres_ref, pre_ref, win_hbm, cw_ref, cb_ref, wp_hbm, ps_ref, wout_hbm,
                  post_ref, o_ref, win_ref, wp_ref, wout_ref, wstage_ref, wp32_ref, sem, vext_ref, uext_ref,
                  tmp_ref, stage_ref, yb_ref, so_ref, p_ref, y_ref, *,
                  layer, seq_tile, conv_ch, tiles_per_seq, n_tiles):
    s = pl.program_id(0)
    pre_ref, cb_ref, ps_ref, post_ref = (r.at[pl.ds(layer, 1)] for r in (pre_ref, cb_ref, ps_ref, post_ref))
    win_ref, wout_ref, stage_ref, yb_ref, so_ref, p_ref, y_ref = (
        _used_lanes(r) for r in (win_ref, wout_ref, stage_ref, yb_ref, so_ref, p_ref, y_ref))

    mix = functools.partial(
        _mix_stage, s % tiles_per_seq, x_ref, pre_ref, win_ref, cw_ref, cb_ref, wp_ref, ps_ref,
        vext_ref, uext_ref, tmp_ref, stage_ref, yb_ref, p_ref, y_ref.at[s % 2], ts=seq_tile, a=conv_ch,
        tiles_per_seq=tiles_per_seq)
    out = functools.partial(
        _out_stage, y_ref.at[(s + 1) % 2], xres_ref, wout_ref, post_ref, o_ref, so_ref, ts=seq_tile)

    @pl.when(s == 0)
    def _():
        feeder = _weight_feeder(layer, win_hbm, wout_hbm, wp_hbm, wstage_ref, wp32_ref, sem,
                                win_ref, wp_ref, a=conv_ch)
        _interleave(feeder, mix(first_step=True))

    @pl.when(s == 1)
    def _():
        _finish_out_weights(layer, wout_hbm, wstage_ref, sem, wout_ref, first=len(_in_segment_cols(conv_ch)))

    @pl.when(jnp.logical_and(s > 0, s < n_tiles))
    def _():
        _interleave(mix(first_step=False), out())

    @pl.when(s == n_tiles)
    def _():
        _interleave(out(), iter(()))


def _layer(l, x, pre, win, cw, cb, wp, ps, wout, post):
    bsz, seq, d = x.shape
    a = cw.shape[-1]
    p_ch = ps.shape[-1]
    pool_gc = wp.shape[-1]
    in_cols = win.shape[-1]
    ts = SEQ_TILE
    assert seq % ts == 0 and ts % OUT_ROWS == 0
    assert in_cols == 4 * a + 2 * p_ch and p_ch == N_POOL_GROUPS * pool_gc
    assert pool_gc == LANES and a % LANES == 0 and p_ch == a and d == a + p_ch and d % a == 0
    tiles_per_seq = seq // ts
    n_tiles = bsz * tiles_per_seq

    def tile_index(t):
        return (t // tiles_per_seq, t % tiles_per_seq, 0)

    mix_tile = lambda s: tile_index(jnp.minimum(s, n_tiles - 1))
    out_tile = lambda s: tile_index(jnp.maximum(s - 1, 0))

    def stacked_rows(width):
        return pl.BlockSpec((pre.shape[0], width), lambda s: (0, 0))

    step_kernel = functools.partial(_layer_kernel, layer=l, seq_tile=ts, conv_ch=a,
                                    tiles_per_seq=tiles_per_seq, n_tiles=n_tiles)

    def layer_call(x_hbm, pre_hbm, win_hbm, cw_hbm, cb_hbm, wp_hbm, ps_hbm, wout_hbm, post_hbm, o_hbm,
                   *scratch):
        def step(x_ref, xres_ref, pre_ref, cw_ref, cb_ref, ps_ref, post_ref, o_ref, *scratch_refs):
            step_kernel(x_ref, xres_ref, pre_ref, win_hbm, cw_ref, cb_ref, wp_hbm, ps_ref, wout_hbm,
                        post_ref, o_ref, *scratch_refs)

        pltpu.emit_pipeline(
            step,
            grid=(n_tiles + 1,),
            in_specs=[
                pl.BlockSpec((1, ts, d), mix_tile),
                pl.BlockSpec((1, ts, d), out_tile),
                stacked_rows(d),
                pl.BlockSpec((None, CONV_K, a), lambda s: (l, 0, 0)),
                stacked_rows(a),
                stacked_rows(p_ch),
                stacked_rows(d),
            ],
            out_specs=[pl.BlockSpec((1, ts, d), out_tile)],
        )(x_hbm, x_hbm, pre_hbm, cw_hbm, cb_hbm, ps_hbm, post_hbm, o_hbm, scratches=scratch)

    return pl.pallas_call(
        layer_call,
        in_specs=[pl.BlockSpec(memory_space=pl.ANY)] * 9,
        out_specs=pl.BlockSpec(memory_space=pl.ANY),
        out_shape=jax.ShapeDtypeStruct(x.shape, x.dtype),
        scratch_shapes=[
            pltpu.VMEM((d, in_cols + PITCH_PAD), jnp.bfloat16),
            pltpu.VMEM((N_POOL_GROUPS // 2, 2 * pool_gc, 2 * pool_gc), jnp.bfloat16),
            pltpu.VMEM((a + p_ch, d + PITCH_PAD), jnp.bfloat16),
            pltpu.VMEM((WSTAGE_SLOTS, d, a), jnp.float32),
            pltpu.VMEM((N_POOL_GROUPS, pool_gc, pool_gc), jnp.float32),
            pltpu.SemaphoreType.DMA((WSTAGE_SLOTS + 1,)),
            pltpu.VMEM((a // LANES, CONV_HALO + ts, LANES), jnp.float32),
            pltpu.VMEM((N_POOL_GROUPS, POOL_PAD + POOL_HALO + ts, LANES), jnp.float32),
            pltpu.VMEM((POOL_TMPS, POOL_PAD + POOL_HALO + ts, LANES), jnp.float32),
            pltpu.VMEM((N_STAGE, ts, a + PITCH_PAD), jnp.float32),
            pltpu.VMEM((ts, p_ch + PITCH_PAD), jnp.float32),
            pltpu.VMEM((ts, d + PITCH_PAD), jnp.float32),
            pltpu.VMEM((ts, p_ch + PITCH_PAD), jnp.bfloat16),
            pltpu.VMEM((2, ts, a + p_ch + PITCH_PAD), jnp.bfloat16),
        ],
        compiler_params=pltpu.CompilerParams(vmem_limit_bytes=VMEM_LIMIT_BYTES),
        name="sandwich_layer",
    )(x, pre, win, cw, cb, wp, ps, wout, post)


def kernel(x, pre_norm, w_in, conv_w, conv_b, w_pool, pool_scale, w_out, post_norm):
    depth = w_in.shape[0]
    for l in range(depth):
        x = _layer(l, x, pre_norm, w_in, conv_w, conv_b, w_pool, pool_scale, w_out, post_norm)
    return x
```

```python
import functools

import jax
import jax.numpy as jnp
from jax import lax
from jax.experimental import pallas as pl
from jax.experimental.pallas import tpu as pltpu

RMS_EPS = 1e-6
CONV_K = 3
POOL_WINDOWS = (2, 4, 8, 16)
N_POOL_GROUPS = len(POOL_WINDOWS)

SEQ_TILE = 512
OUT_ROWS = 256
LANES = 128
CONV_HALO = 8
POOL_HALO = 16
POOL_PAD = 8
POOL_TMPS = 5
WSTAGE_SLOTS = 2
PITCH_PAD = LANES
VMEM_LIMIT_BYTES = 58 * 1024 * 1024


def _rms_inv(v):
    return lax.rsqrt(jnp.mean(v * v, axis=-1, keepdims=True) + RMS_EPS)


def _silu_of_twice(h):
    return h * jnp.tanh(h) + h


def _shifted_rows(ref, start, rows):
    return ref[pl.ds(start, rows, stride=1), :]


ST_U, ST_C, ST_H, ST_ZA, ST_ZB, ST_B = range(6)
N_STAGE = 6


def _mix_stage(j, x_ref, pre_ref, win_ref, cw_ref, cb_ref, wp_ref, ps_ref,
               vext_ref, uext_ref, tmp_ref, stage_ref, yb_ref, p_ref, y_ref, *,
               ts, a, tiles_per_seq, first_step):
    p_ch = N_POOL_GROUPS * LANES
    body = POOL_PAD + POOL_HALO
    ext = POOL_HALO + ts
    if first_step:
        vext_ref[:, 0:CONV_HALO, :] = jnp.zeros((a // LANES, CONV_HALO, LANES), jnp.float32)
        uext_ref[:, 0:body, :] = jnp.zeros((N_POOL_GROUPS, body, LANES), jnp.float32)
        tmp_ref[:, 0:POOL_PAD, :] = jnp.zeros((tmp_ref.shape[0], POOL_PAD, LANES), jnp.float32)
    carry_on = j + 1 < tiles_per_seq

    x = x_ref[0]
    xg = (x * pre_ref[...]).astype(jnp.bfloat16)

    def proj_to(slot, lo):
        stage_ref[slot] = jnp.dot(xg, win_ref[:, lo:lo + a], preferred_element_type=jnp.float32)

    proj_to(ST_U, 4 * a)
    yield
    inv = _rms_inv(x)
    row = lax.broadcasted_iota(jnp.int32, (POOL_HALO, LANES), 0)
    pos1 = (j * ts + row + 1).astype(jnp.float32)
    n_tmp = 0
    for g, w in enumerate(POOL_WINDOWS):
        lo = g * LANES
        ug = stage_ref[ST_U, :, lo:lo + LANES] * inv
        ue = uext_ref.at[g]
        ue[body:body + ts, :] = ug
        s = ue[POOL_PAD:POOL_PAD + ext, :] + _shifted_rows(ue, POOL_PAD - 1, ext)
        m = 2
        while m < w and m < 8:
            t = tmp_ref.at[n_tmp]
            n_tmp += 1
            t[POOL_PAD:POOL_PAD + ext, :] = s
            s = s + _shifted_rows(t, POOL_PAD - m, ext)
            m *= 2
        if m < w:
            s = s + jnp.concatenate([s[:8], s[:-8]], axis=0)
        s = s[POOL_HALO:]
        p_body = s * (1.0 / w) - ug
        p_head = s[0:POOL_HALO] / jnp.minimum(pos1, float(w)) - ug[0:POOL_HALO]
        p = jnp.concatenate([p_head, p_body[POOL_HALO:]], axis=0)
        p_ref[:, lo:lo + LANES] = p.astype(jnp.bfloat16)
        ue[POOL_PAD:body, :] = jnp.where(carry_on, ug[ts - POOL_HALO:], 0.0)

    proj_to(ST_C, a)
    yield
    proj_to(ST_H, 2 * a)
    yield
    inv2 = inv * inv
    half_inv = 0.5 * inv
    conv = []
    for q in range(a // LANES):
        lo = q * LANES
        vq = (stage_ref[ST_C, :, lo:lo + LANES] * stage_ref[ST_H, :, lo:lo + LANES]) * inv2
        ve = vext_ref.at[q]
        ve[CONV_HALO:CONV_HALO + ts, :] = vq
        conv.append((cw_ref[0:1, lo:lo + LANES] * _shifted_rows(ve, CONV_HALO - 2, ts)
                     + cw_ref[1:2, lo:lo + LANES] * _shifted_rows(ve, CONV_HALO - 1, ts)
                     + cw_ref[2:3, lo:lo + LANES] * vq) + cb_ref[:, lo:lo + LANES])
        ve[0:CONV_HALO, :] = jnp.where(carry_on, vq[ts - CONV_HALO:], 0.0)
    conv = jnp.concatenate(conv, axis=1)
    proj_to(ST_ZA, 3 * a)
    yield
    stage_ref[ST_ZA] = conv * _silu_of_twice(stage_ref[ST_ZA] * half_inv)
    proj_to(ST_ZB, 4 * a + p_ch)
    yield
    stage_ref[ST_ZB] = _silu_of_twice(stage_ref[ST_ZB] * half_inv) * ps_ref[...]
    proj_to(ST_B, 0)
    yield
    y_ref[:, 0:a] = ((stage_ref[ST_B] * inv) * stage_ref[ST_ZA]).astype(jnp.bfloat16)

    for hh in range(wp_ref.shape[0]):
        lo = hh * 2 * LANES
        yb_ref[:, lo:lo + 2 * LANES] = jnp.dot(p_ref[:, lo:lo + 2 * LANES], wp_ref[hh],
                                               preferred_element_type=jnp.float32)
    yield
    y_ref[:, a:a + p_ch] = (yb_ref[...] * stage_ref[ST_ZB]).astype(jnp.bfloat16)


def _out_stage(y_ref, xres_ref, wout_ref, post_ref, o_ref, so_ref, *, ts):
    for r in range(0, ts, OUT_ROWS):
        so_ref[r:r + OUT_ROWS, :] = jnp.dot(y_ref[r:r + OUT_ROWS, :], wout_ref[...],
                                            preferred_element_type=jnp.float32)
        yield
        out = so_ref[r:r + OUT_ROWS, :]
        o_ref[0, r:r + OUT_ROWS, :] = xres_ref[0, r:r + OUT_ROWS, :] + (out * _rms_inv(out)) * post_ref[...]


def _interleave(first, second):
    pending = [first, second]
    while pending:
        for gen in list(pending):
            try:
                next(gen)
            except StopIteration:
                pending.remove(gen)


def _weight_feeder(layer, win_hbm, wout_hbm, wp_hbm, wstage_ref, wp32_ref, sem, win_ref, wp_ref, *, a):
    in_cols = _in_segment_cols(a)
    copies = [_segment_copy(win_hbm, layer, lo, wstage_ref, sem, k) for k, lo in enumerate(in_cols)]
    copies += _out_weight_copies(wout_hbm, layer, wstage_ref, sem, first=len(in_cols))
    wp_copy = pltpu.make_async_copy(wp_hbm.at[layer], wp32_ref, sem.at[WSTAGE_SLOTS])
    copies[0].start()
    copies[1].start()
    wp_copy.start()
    for k, lo in enumerate(in_cols):
        copies[k].wait()
        win_ref[:, lo:lo + a] = wstage_ref[k % WSTAGE_SLOTS].astype(jnp.bfloat16)
        copies[k + WSTAGE_SLOTS].start()
        yield
    wp_copy.wait()
    c = wp32_ref.shape[-1]
    wp_ref[...] = jnp.zeros(wp_ref.shape, jnp.bfloat16)
    for g in range(wp32_ref.shape[0]):
        lo = (g % 2) * c
        wp_ref[g // 2, lo:lo + c, lo:lo + c] = wp32_ref[g].astype(jnp.bfloat16)


def _in_segment_cols(a):
    p_ch = N_POOL_GROUPS * LANES
    return (4 * a, a, 2 * a, 3 * a, 4 * a + p_ch, 0)


def _segment_copy(w_hbm, layer, lo, wstage_ref, sem, k):
    slot = k % WSTAGE_SLOTS
    return pltpu.make_async_copy(w_hbm.at[layer, :, pl.ds(lo, wstage_ref.shape[-1])],
                                 wstage_ref.at[slot], sem.at[slot])


def _out_weight_copies(wout_hbm, layer, wstage_ref, sem, *, first):
    width = wstage_ref.shape[-1]
    return [_segment_copy(wout_hbm, layer, h * width, wstage_ref, sem, first + h)
            for h in range(wout_hbm.shape[-1] // width)]


def _finish_out_weights(layer, wout_hbm, wstage_ref, sem, wout_ref, *, first):
    width = wstage_ref.shape[-1]
    for h, cp in enumerate(_out_weight_copies(wout_hbm, layer, wstage_ref, sem, first=first)):
        cp.wait()
        wout_ref[:, h * width:(h + 1) * width] = wstage_ref[(first + h) % WSTAGE_SLOTS].astype(jnp.bfloat16)


def _used_lanes(ref):
    return ref.at[(slice(None),) * (len(ref.shape) - 1) + (pl.ds(0, ref.shape[-1] - PITCH_PAD),)]


def _layer_kernel(x_ref, xres_ref, pre_ref, win_hbm, cw_ref, cb_ref, wp_hbm, ps_ref, wout_hbm,
                  post_ref, o_ref, win_ref, wp_ref, wout_ref, wstage_ref, wp32_ref, sem, vext_ref, uext_ref,
                  tmp_ref, stage_ref, yb_ref, so_ref, p_ref, y_ref, *,
                  layer, seq_tile, conv_ch, tiles_per_seq, n_tiles):
    s = pl.program_id(0)
    pre_ref, cb_ref, ps_ref, post_ref = (r.at[pl.ds(layer, 1)] for r in (pre_ref, cb_ref, ps_ref, post_ref))
    win_ref, wout_ref, stage_ref, yb_ref, so_ref, p_ref, y_ref = (
        _used_lanes(r) for r in (win_ref, wout_ref, stage_ref, yb_ref, so_ref, p_ref, y_ref))

    mix = functools.partial(
        _mix_stage, s % tiles_per_seq, x_ref, pre_ref, win_ref, cw_ref, cb_ref, wp_ref, ps_ref,
        vext_ref, uext_ref, tmp_ref, stage_ref, yb_ref, p_ref, y_ref.at[s % 2], ts=seq_tile, a=conv_ch,
        tiles_per_seq=tiles_per_seq)
    out = functools.partial(
        _out_stage, y_ref.at[(s + 1) % 2], xres_ref, wout_ref, post_ref, o_ref, so_ref, ts=seq_tile)

    @pl.when(s == 0)
    def _():
        feeder = _weight_feeder(layer, win_hbm, wout_hbm, wp_hbm, wstage_ref, wp32_ref, sem,
                                win_ref, wp_ref, a=conv_ch)
        _interleave(feeder, mix(first_step=True))

    @pl.when(s == 1)
    def _():
        _finish_out_weights(layer, wout_hbm, wstage_ref, sem, wout_ref, first=len(_in_segment_cols(conv_ch)))

    @pl.when(jnp.logical_and(s > 0, s < n_tiles))
    def _():
        _interleave(mix(first_step=False), out())

    @pl.when(s == n_tiles)
    def _():
        _interleave(out(), iter(()))


def _sandwich(x, pre, win, cw, cb, wp, ps, wout, post):
    depth = win.shape[0]
    bsz, seq, d = x.shape
    a = cw.shape[-1]
    p_ch = ps.shape[-1]
    pool_gc = wp.shape[-1]
    in_cols = win.shape[-1]
    ts = SEQ_TILE
    assert seq % ts == 0 and ts % OUT_ROWS == 0
    assert in_cols == 4 * a + 2 * p_ch and p_ch == N_POOL_GROUPS * pool_gc
    assert pool_gc == LANES and a % LANES == 0 and p_ch == a and d == a + p_ch and d % a == 0
    tiles_per_seq = seq // ts
    n_tiles = bsz * tiles_per_seq

    def tile_index(t):
        return (t // tiles_per_seq, t % tiles_per_seq, 0)

    mix_tile = lambda s: tile_index(jnp.minimum(s, n_tiles - 1))
    out_tile = lambda s: tile_index(jnp.maximum(s - 1, 0))

    def stacked_rows(width):
        return pl.BlockSpec((pre.shape[0], width), lambda s: (0, 0))

    def sandwich_call(x_hbm, pre_hbm, win_hbm, cw_hbm, cb_hbm, wp_hbm, ps_hbm, wout_hbm, post_hbm, *rest):
        layer_out, scratch = rest[:depth], rest[depth:]
        src_hbm = x_hbm
        for l in range(depth):
            step_kernel = functools.partial(_layer_kernel, layer=l, seq_tile=ts, conv_ch=a,
                                            tiles_per_seq=tiles_per_seq, n_tiles=n_tiles)

            def step(x_ref, xres_ref, pre_ref, cw_ref, cb_ref, ps_ref, post_ref, o_ref, *scratch_refs,
                     step_kernel=step_kernel):
                step_kernel(x_ref, xres_ref, pre_ref, win_hbm, cw_ref, cb_ref, wp_hbm, ps_ref, wout_hbm,
                            post_ref, o_ref, *scratch_refs)

            pltpu.emit_pipeline(
                step,
                grid=(n_tiles + 1,),
                in_specs=[
                    pl.BlockSpec((1, ts, d), mix_tile),
                    pl.BlockSpec((1, ts, d), out_tile),
                    stacked_rows(d),
                    pl.BlockSpec((None, CONV_K, a), lambda s, l=l: (l, 0, 0)),
                    stacked_rows(a),
                    stacked_rows(p_ch),
                    stacked_rows(d),
                ],
                out_specs=[pl.BlockSpec((1, ts, d), out_tile)],
            )(src_hbm, src_hbm, pre_hbm, cw_hbm, cb_hbm, ps_hbm, post_hbm, layer_out[l], scratches=scratch)
            src_hbm = layer_out[l]

    return pl.pallas_call(
        sandwich_call,
        in_specs=[pl.BlockSpec(memory_space=pl.ANY)] * 9,
        out_specs=[pl.BlockSpec(memory_space=pl.ANY)] * depth,
        out_shape=[jax.ShapeDtypeStruct(x.shape, x.dtype)] * depth,
        scratch_shapes=[
            pltpu.VMEM((d, in_cols + PITCH_PAD), jnp.bfloat16),
            pltpu.VMEM((N_POOL_GROUPS // 2, 2 * pool_gc, 2 * pool_gc), jnp.bfloat16),
            pltpu.VMEM((a + p_ch, d + PITCH_PAD), jnp.bfloat16),
            pltpu.VMEM((WSTAGE_SLOTS, d, a), jnp.float32),
            pltpu.VMEM((N_POOL_GROUPS, pool_gc, pool_gc), jnp.float32),
            pltpu.SemaphoreType.DMA((WSTAGE_SLOTS + 1,)),
            pltpu.VMEM((a // LANES, CONV_HALO + ts, LANES), jnp.float32),
            pltpu.VMEM((N_POOL_GROUPS, POOL_PAD + POOL_HALO + ts, LANES), jnp.float32),
            pltpu.VMEM((POOL_TMPS, POOL_PAD + POOL_HALO + ts, LANES), jnp.float32),
            pltpu.VMEM((N_STAGE, ts, a + PITCH_PAD), jnp.float32),
            pltpu.VMEM((ts, p_ch + PITCH_PAD), jnp.float32),
            pltpu.VMEM((ts, d + PITCH_PAD), jnp.float32),
            pltpu.VMEM((ts, p_ch + PITCH_PAD), jnp.bfloat16),
            pltpu.VMEM((2, ts, a + p_ch + PITCH_PAD), jnp.bfloat16),
        ],
        compiler_params=pltpu.CompilerParams(vmem_limit_bytes=VMEM_LIMIT_BYTES),
        name="sandwich_layers",
    )(x, pre, win, cw, cb, wp, ps, wout, post)[-1]


def kernel(x, pre_norm, w_in, conv_w, conv_b, w_pool, pool_scale, w_out, post_norm):
    return _sandwich(x, pre_norm, w_in, conv_w, conv_b, w_pool, pool_scale, w_out, post_norm)
```

```python
import functools

import jax
import jax.numpy as jnp
from jax import lax
from jax.experimental import pallas as pl
from jax.experimental.pallas import tpu as pltpu

RMS_EPS = 1e-6
CONV_K = 3
POOL_WINDOWS = (2, 4, 8, 16)
N_POOL_GROUPS = len(POOL_WINDOWS)

SEQ_TILE = 512
OUT_ROWS = 256
LANES = 128
CONV_HALO = 8
POOL_HALO = 16
POOL_PAD = 8
POOL_TMPS = 5
WSTAGE_SLOTS = 2
PITCH_PAD = LANES
VMEM_LIMIT_BYTES = 58 * 1024 * 1024


def _rms_inv(v):
    return lax.rsqrt(jnp.mean(v * v, axis=-1, keepdims=True) + RMS_EPS)


def _silu_of_twice(h):
    return h * jnp.tanh(h) + h


def _shifted_rows(ref, start, rows):
    return ref[pl.ds(start, rows, stride=1), :]


ST_U, ST_C, ST_H, ST_ZA, ST_ZB, ST_B = range(6)
N_STAGE = 6


def _mix_stage(j, x_ref, pre_ref, win_ref, cw_ref, cb_ref, wp_ref, ps_ref,
               vext_ref, uext_ref, tmp_ref, stage_ref, yb_ref, p_ref, y_ref, *,
               ts, a, tiles_per_seq, first_step):
    p_ch = N_POOL_GROUPS * LANES
    body = POOL_PAD + POOL_HALO
    ext = POOL_HALO + ts
    if first_step:
        vext_ref[:, 0:CONV_HALO, :] = jnp.zeros((a // LANES, CONV_HALO, LANES), jnp.float32)
        uext_ref[:, 0:body, :] = jnp.zeros((N_POOL_GROUPS, body, LANES), jnp.float32)
        tmp_ref[:, 0:POOL_PAD, :] = jnp.zeros((tmp_ref.shape[0], POOL_PAD, LANES), jnp.float32)
    carry_on = j + 1 < tiles_per_seq

    x = x_ref[0]
    xg = (x * pre_ref[...]).astype(jnp.bfloat16)

    def proj_to(slot, lo):
        stage_ref[slot] = jnp.dot(xg, win_ref[:, lo:lo + a], preferred_element_type=jnp.float32)

    proj_to(ST_U, 4 * a)
    yield
    inv = _rms_inv(x)
    row = lax.broadcasted_iota(jnp.int32, (POOL_HALO, LANES), 0)
    pos1 = (j * ts + row + 1).astype(jnp.float32)
    n_tmp = 0
    for g, w in enumerate(POOL_WINDOWS):
        lo = g * LANES
        ug = stage_ref[ST_U, :, lo:lo + LANES] * inv
        ue = uext_ref.at[g]
        ue[body:body + ts, :] = ug
        s = ue[POOL_PAD:POOL_PAD + ext, :] + _shifted_rows(ue, POOL_PAD - 1, ext)
        m = 2
        while m < w and m < 8:
            t = tmp_ref.at[n_tmp]
            n_tmp += 1
            t[POOL_PAD:POOL_PAD + ext, :] = s
            s = s + _shifted_rows(t, POOL_PAD - m, ext)
            m *= 2
        if m < w:
            s = s + jnp.concatenate([s[:8], s[:-8]], axis=0)
        s = s[POOL_HALO:]
        p_body = s * (1.0 / w) - ug
        p_head = s[0:POOL_HALO] / jnp.minimum(pos1, float(w)) - ug[0:POOL_HALO]
        p = jnp.concatenate([p_head, p_body[POOL_HALO:]], axis=0)
        p_ref[:, lo:lo + LANES] = p.astype(jnp.bfloat16)
        ue[POOL_PAD:body, :] = jnp.where(carry_on, ug[ts - POOL_HALO:], 0.0)

    proj_to(ST_C, a)
    yield
    proj_to(ST_H, 2 * a)
    yield
    inv2 = inv * inv
    half_inv = 0.5 * inv
    conv = []
    for q in range(a // LANES):
        lo = q * LANES
        vq = (stage_ref[ST_C, :, lo:lo + LANES] * stage_ref[ST_H, :, lo:lo + LANES]) * inv2
        ve = vext_ref.at[q]
        ve[CONV_HALO:CONV_HALO + ts, :] = vq
        conv.append((cw_ref[0:1, lo:lo + LANES] * _shifted_rows(ve, CONV_HALO - 2, ts)
                     + cw_ref[1:2, lo:lo + LANES] * _shifted_rows(ve, CONV_HALO - 1, ts)
                     + cw_ref[2:3, lo:lo + LANES] * vq) + cb_ref[:, lo:lo + LANES])
        ve[0:CONV_HALO, :] = jnp.where(carry_on, vq[ts - CONV_HALO:], 0.0)
    conv = jnp.concatenate(conv, axis=1)
    proj_to(ST_ZA, 3 * a)
    yield
    stage_ref[ST_ZA] = conv * _silu_of_twice(stage_ref[ST_ZA] * half_inv)
    proj_to(ST_ZB, 4 * a + p_ch)
    yield
    stage_ref[ST_ZB] = _silu_of_twice(stage_ref[ST_ZB] * half_inv) * ps_ref[...]
    proj_to(ST_B, 0)
    yield
    y_ref[:, 0:a] = ((stage_ref[ST_B] * inv) * stage_ref[ST_ZA]).astype(jnp.bfloat16)

    for hh in range(wp_ref.shape[0]):
        lo = hh * 2 * LANES
        yb_ref[:, lo:lo + 2 * LANES] = jnp.dot(p_ref[:, lo:lo + 2 * LANES], wp_ref[hh],
                                               preferred_element_type=jnp.float32)
    yield
    y_ref[:, a:a + p_ch] = (yb_ref[...] * stage_ref[ST_ZB]).astype(jnp.bfloat16)


def _out_stage(y_ref, xres_ref, wout_ref, post_ref, o_ref, so_ref, *, ts):
    for r in range(0, ts, OUT_ROWS):
        so_ref[r:r + OUT_ROWS, :] = jnp.dot(y_ref[r:r + OUT_ROWS, :], wout_ref[...],
                                            preferred_element_type=jnp.float32)
        yield
        out = so_ref[r:r + OUT_ROWS, :]
        o_ref[0, r:r + OUT_ROWS, :] = xres_ref[0, r:r + OUT_ROWS, :] + (out * _rms_inv(out)) * post_ref[...]


def _interleave(first, second):
    pending = [first, second]
    while pending:
        for gen in list(pending):
            try:
                next(gen)
            except StopIteration:
                pending.remove(gen)


def _first_weight_copies(layer, win_hbm, wp_hbm, wstage_ref, wp32_ref, sem, *, a):
    in_cols = _in_segment_cols(a)
    return ([_segment_copy(win_hbm, layer, in_cols[k], wstage_ref, sem, k) for k in range(WSTAGE_SLOTS)]
            + [pltpu.make_async_copy(wp_hbm.at[layer], wp32_ref, sem.at[WSTAGE_SLOTS])])


def _weight_feeder(layer, win_hbm, wout_hbm, wp_hbm, wstage_ref, wp32_ref, sem, win_ref, wp_ref, *, a,
                   prefetched):
    in_cols = _in_segment_cols(a)
    copies = [_segment_copy(win_hbm, layer, lo, wstage_ref, sem, k) for k, lo in enumerate(in_cols)]
    copies += _out_weight_copies(wout_hbm, layer, wstage_ref, sem, first=len(in_cols))
    wp_copy = pltpu.make_async_copy(wp_hbm.at[layer], wp32_ref, sem.at[WSTAGE_SLOTS])
    if not prefetched:
        for cp in _first_weight_copies(layer, win_hbm, wp_hbm, wstage_ref, wp32_ref, sem, a=a):
            cp.start()
    for k, lo in enumerate(in_cols):
        copies[k].wait()
        win_ref[:, lo:lo + a] = wstage_ref[k % WSTAGE_SLOTS].astype(jnp.bfloat16)
        copies[k + WSTAGE_SLOTS].start()
        yield
    wp_copy.wait()
    c = wp32_ref.shape[-1]
    wp_ref[...] = jnp.zeros(wp_ref.shape, jnp.bfloat16)
    for g in range(wp32_ref.shape[0]):
        lo = (g % 2) * c
        wp_ref[g // 2, lo:lo + c, lo:lo + c] = wp32_ref[g].astype(jnp.bfloat16)


def _in_segment_cols(a):
    p_ch = N_POOL_GROUPS * LANES
    return (4 * a, a, 2 * a, 3 * a, 4 * a + p_ch, 0)


def _segment_copy(w_hbm, layer, lo, wstage_ref, sem, k):
    slot = k % WSTAGE_SLOTS
    return pltpu.make_async_copy(w_hbm.at[layer, :, pl.ds(lo, wstage_ref.shape[-1])],
                                 wstage_ref.at[slot], sem.at[slot])


def _out_weight_copies(wout_hbm, layer, wstage_ref, sem, *, first):
    width = wstage_ref.shape[-1]
    return [_segment_copy(wout_hbm, layer, h * width, wstage_ref, sem, first + h)
            for h in range(wout_hbm.shape[-1] // width)]


def _finish_out_weights(layer, wout_hbm, wstage_ref, sem, wout_ref, *, first):
    width = wstage_ref.shape[-1]
    for h, cp in enumerate(_out_weight_copies(wout_hbm, layer, wstage_ref, sem, first=first)):
        cp.wait()
        wout_ref[:, h * width:(h + 1) * width] = wstage_ref[(first + h) % WSTAGE_SLOTS].astype(jnp.bfloat16)


def _used_lanes(ref):
    return ref.at[(slice(None),) * (len(ref.shape) - 1) + (pl.ds(0, ref.shape[-1] - PITCH_PAD),)]


def _layer_kernel(x_ref, xres_ref, pre_ref, win_hbm, cw_ref, cb_ref, wp_hbm, ps_ref, wout_hbm,
                  post_ref, o_ref, win_ref, wp_ref, wout_ref, wstage_ref, wp32_ref, sem, vext_ref, uext_ref,
                  tmp_ref, stage_ref, yb_ref, so_ref, p_ref, y_ref, *,
                  layer, depth, seq_tile, conv_ch, tiles_per_seq, n_tiles):
    s = pl.program_id(0)
    pre_ref, cb_ref, ps_ref, post_ref = (r.at[pl.ds(layer, 1)] for r in (pre_ref, cb_ref, ps_ref, post_ref))
    win_ref, wout_ref, stage_ref, yb_ref, so_ref, p_ref, y_ref = (
        _used_lanes(r) for r in (win_ref, wout_ref, stage_ref, yb_ref, so_ref, p_ref, y_ref))

    mix = functools.partial(
        _mix_stage, s % tiles_per_seq, x_ref, pre_ref, win_ref, cw_ref, cb_ref, wp_ref, ps_ref,
        vext_ref, uext_ref, tmp_ref, stage_ref, yb_ref, p_ref, y_ref.at[s % 2], ts=seq_tile, a=conv_ch,
        tiles_per_seq=tiles_per_seq)
    out = functools.partial(
        _out_stage, y_ref.at[(s + 1) % 2], xres_ref, wout_ref, post_ref, o_ref, so_ref, ts=seq_tile)

    @pl.when(s == 0)
    def _():
        feeder = _weight_feeder(layer, win_hbm, wout_hbm, wp_hbm, wstage_ref, wp32_ref, sem,
                                win_ref, wp_ref, a=conv_ch, prefetched=layer > 0)
        _interleave(feeder, mix(first_step=True))

    @pl.when(s == 1)
    def _():
        _finish_out_weights(layer, wout_hbm, wstage_ref, sem, wout_ref, first=len(_in_segment_cols(conv_ch)))

    if layer + 1 < depth:
        @pl.when(s == 2)
        def _():
            for cp in _first_weight_copies(layer + 1, win_hbm, wp_hbm, wstage_ref, wp32_ref, sem, a=conv_ch):
                cp.start()

    @pl.when(jnp.logical_and(s > 0, s < n_tiles))
    def _():
        _interleave(mix(first_step=False), out())

    @pl.when(s == n_tiles)
    def _():
        _interleave(out(), iter(()))


def _sandwich(x, pre, win, cw, cb, wp, ps, wout, post):
    depth = win.shape[0]
    bsz, seq, d = x.shape
    a = cw.shape[-1]
    p_ch = ps.shape[-1]
    pool_gc = wp.shape[-1]
    in_cols = win.shape[-1]
    ts = SEQ_TILE
    assert seq % ts == 0 and ts % OUT_ROWS == 0
    assert in_cols == 4 * a + 2 * p_ch and p_ch == N_POOL_GROUPS * pool_gc
    assert pool_gc == LANES and a % LANES == 0 and p_ch == a and d == a + p_ch and d % a == 0
    tiles_per_seq = seq // ts
    n_tiles = bsz * tiles_per_seq

    def tile_index(t):
        return (t // tiles_per_seq, t % tiles_per_seq, 0)

    mix_tile = lambda s: tile_index(jnp.minimum(s, n_tiles - 1))
    out_tile = lambda s: tile_index(jnp.maximum(s - 1, 0))

    def stacked_rows(width):
        return pl.BlockSpec((pre.shape[0], width), lambda s: (0, 0))

    def sandwich_call(x_hbm, pre_hbm, win_hbm, cw_hbm, cb_hbm, wp_hbm, ps_hbm, wout_hbm, post_hbm, *rest):
        layer_out, scratch = rest[:depth], rest[depth:]
        src_hbm = x_hbm
        for l in range(depth):
            step_kernel = functools.partial(_layer_kernel, layer=l, depth=depth, seq_tile=ts, conv_ch=a,
                                            tiles_per_seq=tiles_per_seq, n_tiles=n_tiles)

            def step(x_ref, xres_ref, pre_ref, cw_ref, cb_ref, ps_ref, post_ref, o_ref, *scratch_refs,
                     step_kernel=step_kernel):
                step_kernel(x_ref, xres_ref, pre_ref, win_hbm, cw_ref, cb_ref, wp_hbm, ps_ref, wout_hbm,
                            post_ref, o_ref, *scratch_refs)

            pltpu.emit_pipeline(
                step,
                grid=(n_tiles + 1,),
                in_specs=[
                    pl.BlockSpec((1, ts, d), mix_tile),
                    pl.BlockSpec((1, ts, d), out_tile),
                    stacked_rows(d),
                    pl.BlockSpec((None, CONV_K, a), lambda s, l=l: (l, 0, 0)),
                    stacked_rows(a),
                    stacked_rows(p_ch),
                    stacked_rows(d),
                ],
                out_specs=[pl.BlockSpec((1, ts, d), out_tile)],
            )(src_hbm, src_hbm, pre_hbm, cw_hbm, cb_hbm, ps_hbm, post_hbm, layer_out[l], scratches=scratch)
            src_hbm = layer_out[l]

    return pl.pallas_call(
        sandwich_call,
        in_specs=[pl.BlockSpec(memory_space=pl.ANY)] * 9,
        out_specs=[pl.BlockSpec(memory_space=pl.ANY)] * depth,
        out_shape=[jax.ShapeDtypeStruct(x.shape, x.dtype)] * depth,
        scratch_shapes=[
            pltpu.VMEM((d, in_cols + PITCH_PAD), jnp.bfloat16),
            pltpu.VMEM((N_POOL_GROUPS // 2, 2 * pool_gc, 2 * pool_gc), jnp.bfloat16),
            pltpu.VMEM((a + p_ch, d + PITCH_PAD), jnp.bfloat16),
            pltpu.VMEM((WSTAGE_SLOTS, d, a), jnp.float32),
            pltpu.VMEM((N_POOL_GROUPS, pool_gc, pool_gc), jnp.float32),
            pltpu.SemaphoreType.DMA((WSTAGE_SLOTS + 1,)),
            pltpu.VMEM((a // LANES, CONV_HALO + ts, LANES), jnp.float32),
            pltpu.VMEM((N_POOL_GROUPS, POOL_PAD + POOL_HALO + ts, LANES), jnp.float32),
            pltpu.VMEM((POOL_TMPS, POOL_PAD + POOL_HALO + ts, LANES), jnp.float32),
            pltpu.VMEM((N_STAGE, ts, a + PITCH_PAD), jnp.float32),
            pltpu.VMEM((ts, p_ch + PITCH_PAD), jnp.float32),
            pltpu.VMEM((ts, d + PITCH_PAD), jnp.float32),
            pltpu.VMEM((ts, p_ch + PITCH_PAD), jnp.bfloat16),
            pltpu.VMEM((2, ts, a + p_ch + PITCH_PAD), jnp.bfloat16),
        ],
        compiler_params=pltpu.CompilerParams(vmem_limit_bytes=VMEM_LIMIT_BYTES),
        name="sandwich_layers",
    )(x, pre, win, cw, cb, wp, ps, wout, post)[-1]


def kernel(x, pre_norm, w_in, conv_w, conv_b, w_pool, pool_scale, w_out, post_norm):
    return _sandwich(x, pre_norm, w_in, conv_w, conv_b, w_pool, pool_scale, w_out, post_norm)
```

```python
import functools

import jax
import jax.numpy as jnp
from jax import lax
from jax.experimental import pallas as pl
from jax.experimental.pallas import tpu as pltpu

RMS_EPS = 1e-6
CONV_K = 3
POOL_WINDOWS = (2, 4, 8, 16)
N_POOL_GROUPS = len(POOL_WINDOWS)

SEQ_TILE = 512
OUT_ROWS = 256
LANES = 128
CONV_HALO = 8
POOL_HALO = 16
POOL_PAD = 8
POOL_TMPS = 5
WSTAGE_SLOTS = 2
WEIGHT_DMA_PRIORITY = 1
PITCH_PAD = LANES
VMEM_LIMIT_BYTES = 58 * 1024 * 1024


def _rms_inv(v):
    return lax.rsqrt(jnp.mean(v * v, axis=-1, keepdims=True) + RMS_EPS)


def _silu_of_twice(h):
    return h * jnp.tanh(h) + h


def _shifted_rows(ref, start, rows):
    return ref[pl.ds(start, rows, stride=1), :]


ST_U, ST_C, ST_H, ST_ZA, ST_ZB, ST_B = range(6)
N_STAGE = 6


def _mix_stage(j, x_ref, pre_ref, win_ref, cw_ref, cb_ref, wp_ref, ps_ref,
               vext_ref, uext_ref, tmp_ref, stage_ref, yb_ref, p_ref, y_ref, *,
               ts, a, tiles_per_seq, first_step):
    p_ch = N_POOL_GROUPS * LANES
    body = POOL_PAD + POOL_HALO
    ext = POOL_HALO + ts
    if first_step:
        vext_ref[:, 0:CONV_HALO, :] = jnp.zeros((a // LANES, CONV_HALO, LANES), jnp.float32)
        uext_ref[:, 0:body, :] = jnp.zeros((N_POOL_GROUPS, body, LANES), jnp.float32)
        tmp_ref[:, 0:POOL_PAD, :] = jnp.zeros((tmp_ref.shape[0], POOL_PAD, LANES), jnp.float32)
    carry_on = j + 1 < tiles_per_seq

    x = x_ref[0]
    xg = (x * pre_ref[...]).astype(jnp.bfloat16)

    def proj_to(slot, lo):
        stage_ref[slot] = jnp.dot(xg, win_ref[:, lo:lo + a], preferred_element_type=jnp.float32)

    proj_to(ST_U, 4 * a)
    yield
    inv = _rms_inv(x)
    row = lax.broadcasted_iota(jnp.int32, (POOL_HALO, LANES), 0)
    pos1 = (j * ts + row + 1).astype(jnp.float32)
    n_tmp = 0
    for g, w in enumerate(POOL_WINDOWS):
        lo = g * LANES
        ug = stage_ref[ST_U, :, lo:lo + LANES] * inv
        ue = uext_ref.at[g]
        ue[body:body + ts, :] = ug
        s = ue[POOL_PAD:POOL_PAD + ext, :] + _shifted_rows(ue, POOL_PAD - 1, ext)
        m = 2
        while m < w and m < 8:
            t = tmp_ref.at[n_tmp]
            n_tmp += 1
            t[POOL_PAD:POOL_PAD + ext, :] = s
            s = s + _shifted_rows(t, POOL_PAD - m, ext)
            m *= 2
        if m < w:
            s = s + jnp.concatenate([s[:8], s[:-8]], axis=0)
        s = s[POOL_HALO:]
        p_body = s * (1.0 / w) - ug
        p_head = s[0:POOL_HALO] / jnp.minimum(pos1, float(w)) - ug[0:POOL_HALO]
        p = jnp.concatenate([p_head, p_body[POOL_HALO:]], axis=0)
        p_ref[:, lo:lo + LANES] = p.astype(jnp.bfloat16)
        ue[POOL_PAD:body, :] = jnp.where(carry_on, ug[ts - POOL_HALO:], 0.0)

    proj_to(ST_C, a)
    yield
    proj_to(ST_H, 2 * a)
    yield
    inv2 = inv * inv
    half_inv = 0.5 * inv
    conv = []
    for q in range(a // LANES):
        lo = q * LANES
        vq = (stage_ref[ST_C, :, lo:lo + LANES] * stage_ref[ST_H, :, lo:lo + LANES]) * inv2
        ve = vext_ref.at[q]
        ve[CONV_HALO:CONV_HALO + ts, :] = vq
        conv.append((cw_ref[0:1, lo:lo + LANES] * _shifted_rows(ve, CONV_HALO - 2, ts)
                     + cw_ref[1:2, lo:lo + LANES] * _shifted_rows(ve, CONV_HALO - 1, ts)
                     + cw_ref[2:3, lo:lo + LANES] * vq) + cb_ref[:, lo:lo + LANES])
        ve[0:CONV_HALO, :] = jnp.where(carry_on, vq[ts - CONV_HALO:], 0.0)
    conv = jnp.concatenate(conv, axis=1)
    proj_to(ST_ZA, 3 * a)
    yield
    stage_ref[ST_ZA] = conv * _silu_of_twice(stage_ref[ST_ZA] * half_inv)
    proj_to(ST_ZB, 4 * a + p_ch)
    yield
    stage_ref[ST_ZB] = _silu_of_twice(stage_ref[ST_ZB] * half_inv) * ps_ref[...]
    proj_to(ST_B, 0)
    yield
    y_ref[:, 0:a] = ((stage_ref[ST_B] * inv) * stage_ref[ST_ZA]).astype(jnp.bfloat16)

    for hh in range(wp_ref.shape[0]):
        lo = hh * 2 * LANES
        yb_ref[:, lo:lo + 2 * LANES] = jnp.dot(p_ref[:, lo:lo + 2 * LANES], wp_ref[hh],
                                               preferred_element_type=jnp.float32)
    yield
    y_ref[:, a:a + p_ch] = (yb_ref[...] * stage_ref[ST_ZB]).astype(jnp.bfloat16)


def _out_stage(y_ref, xres_ref, wout_ref, post_ref, o_ref, so_ref, *, ts):
    for r in range(0, ts, OUT_ROWS):
        so_ref[r:r + OUT_ROWS, :] = jnp.dot(y_ref[r:r + OUT_ROWS, :], wout_ref[...],
                                            preferred_element_type=jnp.float32)
        yield
        out = so_ref[r:r + OUT_ROWS, :]
        o_ref[0, r:r + OUT_ROWS, :] = xres_ref[0, r:r + OUT_ROWS, :] + (out * _rms_inv(out)) * post_ref[...]


def _interleave(first, second):
    pending = [first, second]
    while pending:
        for gen in list(pending):
            try:
                next(gen)
            except StopIteration:
                pending.remove(gen)


def _weight_feeder(layer, win_hbm, wout_hbm, wp_hbm, wstage_ref, wp32_ref, sem, win_ref, wp_ref, *, a):
    in_cols = _in_segment_cols(a)
    copies = [_segment_copy(win_hbm, layer, lo, wstage_ref, sem, k) for k, lo in enumerate(in_cols)]
    copies += _out_weight_copies(wout_hbm, layer, wstage_ref, sem, first=len(in_cols))
    wp_copy = pltpu.make_async_copy(wp_hbm.at[layer], wp32_ref, sem.at[WSTAGE_SLOTS])
    copies[0].start(priority=WEIGHT_DMA_PRIORITY)
    copies[1].start(priority=WEIGHT_DMA_PRIORITY)
    wp_copy.start(priority=WEIGHT_DMA_PRIORITY)
    for k, lo in enumerate(in_cols):
        copies[k].wait()
        win_ref[:, lo:lo + a] = wstage_ref[k % WSTAGE_SLOTS].astype(jnp.bfloat16)
        copies[k + WSTAGE_SLOTS].start(priority=WEIGHT_DMA_PRIORITY)
        yield
    wp_copy.wait()
    c = wp32_ref.shape[-1]
    wp_ref[...] = jnp.zeros(wp_ref.shape, jnp.bfloat16)
    for g in range(wp32_ref.shape[0]):
        lo = (g % 2) * c
        wp_ref[g // 2, lo:lo + c, lo:lo + c] = wp32_ref[g].astype(jnp.bfloat16)


def _in_segment_cols(a):
    p_ch = N_POOL_GROUPS * LANES
    return (4 * a, a, 2 * a, 3 * a, 4 * a + p_ch, 0)


def _segment_copy(w_hbm, layer, lo, wstage_ref, sem, k):
    slot = k % WSTAGE_SLOTS
    return pltpu.make_async_copy(w_hbm.at[layer, :, pl.ds(lo, wstage_ref.shape[-1])],
                                 wstage_ref.at[slot], sem.at[slot])


def _out_weight_copies(wout_hbm, layer, wstage_ref, sem, *, first):
    width = wstage_ref.shape[-1]
    return [_segment_copy(wout_hbm, layer, h * width, wstage_ref, sem, first + h)
            for h in range(wout_hbm.shape[-1] // width)]


def _finish_out_weights(layer, wout_hbm, wstage_ref, sem, wout_ref, *, first):
    width = wstage_ref.shape[-1]
    for h, cp in enumerate(_out_weight_copies(wout_hbm, layer, wstage_ref, sem, first=first)):
        cp.wait()
        wout_ref[:, h * width:(h + 1) * width] = wstage_ref[(first + h) % WSTAGE_SLOTS].astype(jnp.bfloat16)


def _used_lanes(ref):
    return ref.at[(slice(None),) * (len(ref.shape) - 1) + (pl.ds(0, ref.shape[-1] - PITCH_PAD),)]


def _layer_kernel(x_ref, xres_ref, pre_ref, win_hbm, cw_ref, cb_ref, wp_hbm, ps_ref, wout_hbm,
                  post_ref, o_ref, win_ref, wp_ref, wout_ref, wstage_ref, wp32_ref, sem, vext_ref, uext_ref,
                  tmp_ref, stage_ref, yb_ref, so_ref, p_ref, y_ref, *,
                  layer, seq_tile, conv_ch, tiles_per_seq, n_tiles):
    s = pl.program_id(0)
    pre_ref, cb_ref, ps_ref, post_ref = (r.at[pl.ds(layer, 1)] for r in (pre_ref, cb_ref, ps_ref, post_ref))
    win_ref, wout_ref, stage_ref, yb_ref, so_ref, p_ref, y_ref = (
        _used_lanes(r) for r in (win_ref, wout_ref, stage_ref, yb_ref, so_ref, p_ref, y_ref))

    mix = functools.partial(
        _mix_stage, s % tiles_per_seq, x_ref, pre_ref, win_ref, cw_ref, cb_ref, wp_ref, ps_ref,
        vext_ref, uext_ref, tmp_ref, stage_ref, yb_ref, p_ref, y_ref.at[s % 2], ts=seq_tile, a=conv_ch,
        tiles_per_seq=tiles_per_seq)
    out = functools.partial(
        _out_stage, y_ref.at[(s + 1) % 2], xres_ref, wout_ref, post_ref, o_ref, so_ref, ts=seq_tile)

    @pl.when(s == 0)
    def _():
        feeder = _weight_feeder(layer, win_hbm, wout_hbm, wp_hbm, wstage_ref, wp32_ref, sem,
                                win_ref, wp_ref, a=conv_ch)
        _interleave(feeder, mix(first_step=True))

    @pl.when(s == 1)
    def _():
        _finish_out_weights(layer, wout_hbm, wstage_ref, sem, wout_ref, first=len(_in_segment_cols(conv_ch)))

    @pl.when(jnp.logical_and(s > 0, s < n_tiles))
    def _():
        _interleave(mix(first_step=False), out())

    @pl.when(s == n_tiles)
    def _():
        _interleave(out(), iter(()))


def _layer(l, x, pre, win, cw, cb, wp, ps, wout, post):
    bsz, seq, d = x.shape
    a = cw.shape[-1]
    p_ch = ps.shape[-1]
    pool_gc = wp.shape[-1]
    in_cols = win.shape[-1]
    ts = SEQ_TILE
    assert seq % ts == 0 and ts % OUT_ROWS == 0
    assert in_cols == 4 * a + 2 * p_ch and p_ch == N_POOL_GROUPS * pool_gc
    assert pool_gc == LANES and a % LANES == 0 and p_ch == a and d == a + p_ch and d % a == 0
    tiles_per_seq = seq // ts
    n_tiles = bsz * tiles_per_seq

    def layer_block(*shape):
        return pl.BlockSpec((None,) + shape, lambda s: (l,) + (0,) * len(shape),
                            pipeline_mode=pl.Buffered(1))

    def tile_index(t):
        return (t // tiles_per_seq, t % tiles_per_seq, 0)

    mix_tile = lambda s: tile_index(jnp.minimum(s, n_tiles - 1))
    out_tile = lambda s: tile_index(jnp.maximum(s - 1, 0))

    def stacked_rows(width):
        return pl.BlockSpec((pre.shape[0], width), lambda s: (0, 0), pipeline_mode=pl.Buffered(1))

    kernel = functools.partial(_layer_kernel, layer=l, seq_tile=ts, conv_ch=a,
                               tiles_per_seq=tiles_per_seq, n_tiles=n_tiles)
    return pl.pallas_call(
        kernel,
        grid=(n_tiles + 1,),
        in_specs=[
            pl.BlockSpec((1, ts, d), mix_tile),
            pl.BlockSpec((1, ts, d), out_tile),
            stacked_rows(d),
            pl.BlockSpec(memory_space=pl.ANY),
            layer_block(CONV_K, a),
            stacked_rows(a),
            pl.BlockSpec(memory_space=pl.ANY),
            stacked_rows(p_ch),
            pl.BlockSpec(memory_space=pl.ANY),
            stacked_rows(d),
        ],
        out_specs=pl.BlockSpec((1, ts, d), out_tile),
        out_shape=jax.ShapeDtypeStruct(x.shape, x.dtype),
        scratch_shapes=[
            pltpu.VMEM((d, in_cols + PITCH_PAD), jnp.bfloat16),
            pltpu.VMEM((N_POOL_GROUPS // 2, 2 * pool_gc, 2 * pool_gc), jnp.bfloat16),
            pltpu.VMEM((a + p_ch, d + PITCH_PAD), jnp.bfloat16),
            pltpu.VMEM((WSTAGE_SLOTS, d, a), jnp.float32),
            pltpu.VMEM((N_POOL_GROUPS, pool_gc, pool_gc), jnp.float32),
            pltpu.SemaphoreType.DMA((WSTAGE_SLOTS + 1,)),
            pltpu.VMEM((a // LANES, CONV_HALO + ts, LANES), jnp.float32),
            pltpu.VMEM((N_POOL_GROUPS, POOL_PAD + POOL_HALO + ts, LANES), jnp.float32),
            pltpu.VMEM((POOL_TMPS, POOL_PAD + POOL_HALO + ts, LANES), jnp.float32),
            pltpu.VMEM((N_STAGE, ts, a + PITCH_PAD), jnp.float32),
            pltpu.VMEM((ts, p_ch + PITCH_PAD), jnp.float32),
            pltpu.VMEM((ts, d + PITCH_PAD), jnp.float32),
            pltpu.VMEM((ts, p_ch + PITCH_PAD), jnp.bfloat16),
            pltpu.VMEM((2, ts, a + p_ch + PITCH_PAD), jnp.bfloat16),
        ],
        compiler_params=pltpu.CompilerParams(
            dimension_semantics=("arbitrary",),
            vmem_limit_bytes=VMEM_LIMIT_BYTES),
        name="sandwich_layer",
    )(x, x, pre, win, cw, cb, wp, ps, wout, post)


def kernel(x, pre_norm, w_in, conv_w, conv_b, w_pool, pool_scale, w_out, post_norm):
    depth = w_in.shape[0]
    for l in range(depth):
        x = _layer(l, x, pre_norm, w_in, conv_w, conv_b, w_pool, pool_scale, w_out, post_norm)
    return x
```
